```python
import jax, jax.numpy as jnp
from jax import lax
import numpy as np

D_MODEL = 1024
BATCH = 8
SEQ = 4096
DEPTH = 1
DEC_BATCH = 128
DEC_SEQ = 4
PAST_LEN = 8192
PAGE_SIZE = 128

GROUPS = ((128, 1), (512, 4), (2048, 16))
N_GROUPS = len(GROUPS)
H_G = 8
HEAD_DIM = 64
ATT_W = H_G * HEAD_DIM
ROPE_THETA = 10000.0
BLOCK = 128
C_CONV = D_MODEL
CONV_WIDTH = 31
D_FF = 2816
N_ATT_COLS = N_GROUPS * 3 * ATT_W
N_IN = N_ATT_COLS + 2 * C_CONV + 2 * D_MODEL
RMS_EPS = 1e-6
LN_EPS = 1e-5
SCALE = HEAD_DIM ** -0.5

kernel_name = 'hybrid_dilated_attn_conformer_conv_macaron_step'


def rms_norm(x, g):
    xf = x.astype(jnp.float32)
    y = xf * lax.rsqrt(jnp.mean(xf * xf, axis=-1, keepdims=True) + RMS_EPS)
    return (y * g.astype(jnp.float32)).astype(x.dtype)


def layer_norm(x, g, b):
    xf = x.astype(jnp.float32)
    mu = jnp.mean(xf, axis=-1, keepdims=True)
    xc = xf - mu
    y = xc * lax.rsqrt(jnp.mean(xc * xc, axis=-1, keepdims=True) + LN_EPS)
    return (y * g.astype(jnp.float32) + b.astype(jnp.float32)).astype(x.dtype)


def half_ffn(x, norm_g, w_gate, w_up, w_down):
    h = rms_norm(x, norm_g)
    return x + 0.5 * ((jax.nn.silu(h @ w_gate) * (h @ w_up)) @ w_down)


def rope(x, positions):
    half = HEAD_DIM // 2
    inv_freq = ROPE_THETA ** (-jnp.arange(half, dtype=jnp.float32) * 2.0 / HEAD_DIM)
    ang = positions.astype(jnp.float32)[:, None] * inv_freq[None, :]
    cos = jnp.cos(ang)[None, :, None, :]
    sin = jnp.sin(ang)[None, :, None, :]
    xf = x.astype(jnp.float32)
    x1, x2 = xf[..., :half], xf[..., half:]
    return jnp.concatenate([x1 * cos - x2 * sin, x2 * cos + x1 * sin], axis=-1).astype(x.dtype)


def _to_class_blocks(t, dil, n_pad):
    b, s, h, e = t.shape
    n = s // dil
    t = t.reshape(b, n, dil, h, e).transpose(0, 2, 1, 3, 4)
    t = jnp.pad(t, ((0, 0), (0, 0), (0, n_pad - n), (0, 0), (0, 0)))
    return t.reshape(b, dil, n_pad // BLOCK, BLOCK, h, e)


def _with_prev_block(t):
    prev = jnp.pad(t[:, :, :-1], ((0, 0), (0, 0), (1, 0), (0, 0), (0, 0), (0, 0)))
    return jnp.concatenate([prev, t], axis=3)


def _from_class_blocks(t, s):
    b, dil = t.shape[:2]
    rest = t.shape[4:]
    t = t.reshape((b, dil, -1) + rest)[:, :, :s // dil]
    t = jnp.swapaxes(t, 1, 2)
    return t.reshape((b, s) + rest)


def dilated_attention_prompt(q, k, v, window, dil):
    s = q.shape[1]
    n = s // dil
    n_pad = -(-n // BLOCK) * BLOCK
    nb = n_pad // BLOCK
    span = window // dil
    qb = _to_class_blocks(q, dil, n_pad)
    kk = _with_prev_block(_to_class_blocks(k, dil, n_pad))
    vv = _with_prev_block(_to_class_blocks(v, dil, n_pad))
    sc = jnp.einsum('bdcqhe,bdckhe->bdchqk', qb, kk).astype(jnp.float32) * SCALE
    qi = jnp.arange(BLOCK)[:, None]
    kj = jnp.arange(2 * BLOCK)[None, :]
    dist = qi + BLOCK - kj
    key_sub = jnp.arange(nb)[:, None, None] * BLOCK - BLOCK + kj[None]
    valid = (dist >= 0) & (dist <= span) & (key_sub >= 0)
    sc = jnp.where(valid[None, None, :, None], sc, -jnp.inf)
    m = jnp.max(sc, axis=-1, keepdims=True)
    p = jnp.exp(sc - m)
    l = jnp.sum(p, axis=-1)
    o = jnp.einsum('bdchqk,bdckhe->bdcqhe', p, vv.astype(jnp.float32)) / jnp.swapaxes(l, 3, 4)[..., None]
    lse = jnp.swapaxes(m[..., 0] + jnp.log(l), 3, 4)
    return _from_class_blocks(o, s), _from_class_blocks(lse, s)


def dilated_attention_sample(q, k_all, v_all, window, dil):
    t = q.shape[1]
    wb = k_all.shape[1] - t
    dists = jnp.arange(window // dil + 1) * dil
    idx = wb + jnp.arange(t)[:, None] - dists[None, :]
    valid = idx >= 0
    idx = jnp.maximum(idx, 0)
    kg = k_all[:, idx]
    vg = v_all[:, idx]
    sc = jnp.einsum('bthe,btkhe->bthk', q, kg).astype(jnp.float32) * SCALE
    sc = jnp.where(valid[None, :, None, :], sc, -jnp.inf)
    m = jnp.max(sc, axis=-1, keepdims=True)
    p = jnp.exp(sc - m)
    l = jnp.sum(p, axis=-1)
    o = jnp.einsum('bthk,btkhe->bthe', p, vg.astype(jnp.float32)) / l[..., None]
    return o, m[..., 0] + jnp.log(l)


def depthwise_causal_conv(u_full, w, b):
    y = lax.conv_general_dilated(u_full, w[:, None, :], window_strides=(1,), padding='VALID',
                                 dimension_numbers=('NWC', 'WIO', 'NWC'),
                                 feature_group_count=u_full.shape[-1])
    return y + b


def token_mixer(xn, positions, kv_bufs, conv_buf, w_in, gate_bias, conv_w, conv_b, conv_ln_g, conv_ln_b,
                w_conv_out, w_att_out, w_o):
    b, t, _ = xn.shape
    proj = xn @ w_in
    qkv = proj[..., :N_ATT_COLS].reshape(b, t, N_GROUPS, 3, H_G, HEAD_DIM)
    glu_a, glu_b, gate_logits = jnp.split(proj[..., N_ATT_COLS:], [C_CONV, 2 * C_CONV], axis=-1)
    outs, lses, new_kv = [], [], []
    for g, (win, dil) in enumerate(GROUPS):
        q = rope(qkv[:, :, g, 0], positions)
        k = rope(qkv[:, :, g, 1], positions)
        v = qkv[:, :, g, 2]
        if kv_bufs is None:
            k_all, v_all = k, v
            o, lse = dilated_attention_prompt(q, k, v, win, dil)
        else:
            k_all = jnp.concatenate([kv_bufs[g][:, :, 0], k], axis=1)
            v_all = jnp.concatenate([kv_bufs[g][:, :, 1], v], axis=1)
            o, lse = dilated_attention_sample(q, k_all, v_all, win, dil)
        keep = min(win, k_all.shape[1])
        new_kv.append(jnp.stack([k_all[:, -keep:], v_all[:, -keep:]], axis=2))
        outs.append(o)
        lses.append(lse)
    w_grp = jax.nn.softmax(jnp.stack(lses, axis=0), axis=0)
    att = jnp.sum(w_grp[..., None] * jnp.stack(outs, axis=0), axis=0)
    a = att.astype(xn.dtype).reshape(b, t, ATT_W) @ w_att_out
    u = glu_a * jax.nn.sigmoid(glu_b)
    prev = jnp.zeros((b, CONV_WIDTH - 1, C_CONV), u.dtype) if conv_buf is None else conv_buf
    u_full = jnp.concatenate([prev, u], axis=1)
    c = depthwise_causal_conv(u_full, conv_w, conv_b)
    c = jax.nn.silu(layer_norm(c, conv_ln_g, conv_ln_b)) @ w_conv_out
    gates = jax.nn.sigmoid(gate_logits + gate_bias)
    g_a, g_c = gates[..., :D_MODEL], gates[..., D_MODEL:]
    y = (g_a * a + g_c * c) @ w_o
    return y, new_kv, u_full[:, -(CONV_WIDTH - 1):]


def setup_inputs(seed: int = 0) -> dict:
    key = jax.random.key(seed)
    ks = iter(jax.random.split(key, 32))

    def nrm(shape, scale):
        return jax.random.normal(next(ks), shape, jnp.float32) * scale

    inp = {}
    inp['x_prompt'] = nrm((BATCH, SEQ, D_MODEL), 1.0)
    inp['x_sample'] = nrm((DEC_BATCH, DEC_SEQ, D_MODEL), 1.0)
    for win, _ in GROUPS:
        inp['cache_kv_w%d' % win] = nrm((DEPTH, DEC_BATCH, min(win, PAST_LEN), 2, H_G, HEAD_DIM), 1.0)
    inp['state_conv'] = nrm((DEPTH, DEC_BATCH, CONV_WIDTH - 1, C_CONV), 0.5)
    inp['ffn1_norm'] = 1.0 + nrm((DEPTH, D_MODEL), 0.01)
    inp['ffn1_w_gate'] = nrm((DEPTH, D_MODEL, D_FF), D_MODEL ** -0.5)
    inp['ffn1_w_up'] = nrm((DEPTH, D_MODEL, D_FF), D_MODEL ** -0.5)
    inp['ffn1_w_down'] = nrm((DEPTH, D_FF, D_MODEL), D_FF ** -0.5)
    inp['mix_norm'] = 1.0 + nrm((DEPTH, D_MODEL), 0.01)
    inp['w_in'] = nrm((DEPTH, D_MODEL, N_IN), D_MODEL ** -0.5)
    inp['gate_bias'] = nrm((DEPTH, 2 * D_MODEL), 0.01)
    inp['conv_w'] = nrm((DEPTH, CONV_WIDTH, C_CONV), CONV_WIDTH ** -0.5)
    inp['conv_b'] = nrm((DEPTH, C_CONV), 0.01)
    inp['conv_ln_g'] = 1.0 + nrm((DEPTH, C_CONV), 0.01)
    inp['conv_ln_b'] = nrm((DEPTH, C_CONV), 0.01)
    inp['w_conv_out'] = nrm((DEPTH, C_CONV, D_MODEL), C_CONV ** -0.5)
    inp['w_att_out'] = nrm((DEPTH, ATT_W, D_MODEL), ATT_W ** -0.5)
    inp['w_o'] = nrm((DEPTH, D_MODEL, D_MODEL), D_MODEL ** -0.5)
    inp['ffn2_norm'] = 1.0 + nrm((DEPTH, D_MODEL), 0.01)
    inp['ffn2_w_gate'] = nrm((DEPTH, D_MODEL, D_FF), D_MODEL ** -0.5)
    inp['ffn2_w_up'] = nrm((DEPTH, D_MODEL, D_FF), D_MODEL ** -0.5)
    inp['ffn2_w_down'] = nrm((DEPTH, D_FF, D_MODEL), D_FF ** -0.5)
    inp['final_norm'] = 1.0 + nrm((D_MODEL,), 0.01)
    return inp


def reference(x_prompt, x_sample, cache_kv_w128, cache_kv_w512, cache_kv_w2048, state_conv,
              ffn1_norm, ffn1_w_gate, ffn1_w_up, ffn1_w_down,
              mix_norm, w_in, gate_bias, conv_w, conv_b, conv_ln_g, conv_ln_b, w_conv_out, w_att_out, w_o,
              ffn2_norm, ffn2_w_gate, ffn2_w_up, ffn2_w_down, final_norm):
    pos_p = jnp.arange(x_prompt.shape[1], dtype=jnp.int32)
    pos_s = PAST_LEN + jnp.arange(x_sample.shape[1], dtype=jnp.int32)
    caches = (cache_kv_w128, cache_kv_w512, cache_kv_w2048)
    xp, xs = x_prompt, x_sample
    kv_p = [[] for _ in GROUPS]
    kv_s = [[] for _ in GROUPS]
    conv_p, conv_s = [], []
    for l in range(DEPTH):
        mix_w = (w_in[l], gate_bias[l], conv_w[l], conv_b[l], conv_ln_g[l], conv_ln_b[l],
                 w_conv_out[l], w_att_out[l], w_o[l])
        ffn1 = (ffn1_norm[l], ffn1_w_gate[l], ffn1_w_up[l], ffn1_w_down[l])
        ffn2 = (ffn2_norm[l], ffn2_w_gate[l], ffn2_w_up[l], ffn2_w_down[l])
        xp = half_ffn(xp, *ffn1)
        xs = half_ffn(xs, *ffn1)
        hp, new_kv, new_conv = token_mixer(rms_norm(xp, mix_norm[l]), pos_p, None, None, *mix_w)
        xp = xp + hp
        for g in range(N_GROUPS):
            kv_p[g].append(new_kv[g])
        conv_p.append(new_conv)
        hs, new_kv, new_conv = token_mixer(rms_norm(xs, mix_norm[l]), pos_s, [c[l] for c in caches],
                                           state_conv[l], *mix_w)
        xs = xs + hs
        for g in range(N_GROUPS):
            kv_s[g].append(new_kv[g])
        conv_s.append(new_conv)
        xp = half_ffn(xp, *ffn2)
        xs = half_ffn(xs, *ffn2)
    y_prompt = rms_norm(xp, final_norm)
    y_sample = rms_norm(xs, final_norm)
    return (y_prompt, y_sample,
            jnp.stack(kv_p[0]), jnp.stack(kv_p[1]), jnp.stack(kv_p[2]), jnp.stack(conv_p),
            jnp.stack(kv_s[0]), jnp.stack(kv_s[1]), jnp.stack(kv_s[2]), jnp.stack(conv_s))
```

```python
import functools

import numpy as np
import jax
import jax.numpy as jnp
from jax import lax
from jax.experimental import pallas as pl
from jax.experimental.pallas import tpu as pltpu

D_MODEL = 1024
D_FF = 2816
GROUPS = ((128, 1), (512, 4), (2048, 16))
N_GROUPS = len(GROUPS)
H_G = 8
HEAD_DIM = 64
ATT_W = H_G * HEAD_DIM
BLOCK = 128
C_CONV = D_MODEL
CONV_WIDTH = 31
N_ATT_COLS = N_GROUPS * 3 * ATT_W
N_IN = N_ATT_COLS + 2 * C_CONV + 2 * D_MODEL
ROPE_THETA = 10000.0
PAST_LEN = 8192
RMS_EPS = 1e-6
LN_EPS = 1e-5
SCALE = HEAD_DIM ** -0.5

LANES = 128
HALO = 32
SAMPLE_Q_ROWS = 8
VMEM_LIMIT = 56 * 1024 * 1024

F32 = jnp.float32
BF16 = jnp.bfloat16


def _params(sem):
    return pltpu.CompilerParams(dimension_semantics=sem, vmem_limit_bytes=VMEM_LIMIT)


def _resident(shape):
    nd = len(shape)
    return pl.BlockSpec(shape, lambda *_: (0,) * nd, pipeline_mode=pl.Buffered(1))


def _rms(x, g):
    return x * lax.rsqrt(jnp.mean(x * x, axis=-1, keepdims=True) + RMS_EPS) * g


FF_CHUNK = 256


def _ffn_kernel(x_ref, g_ref, wg_ref, wu_ref, wd_ref, fin_ref, o_ref, acc_ref, *, final):
    x = x_ref[...]
    h = _rms(x, g_ref[...]).astype(BF16)
    for j in range(D_FF // FF_CHUNK):
        sl = slice(j * FF_CHUNK, (j + 1) * FF_CHUNK)
        gate = jnp.dot(h, wg_ref[:, sl], preferred_element_type=F32)
        up = jnp.dot(h, wu_ref[:, sl], preferred_element_type=F32)
        act = (gate * jax.nn.sigmoid(gate) * up).astype(BF16)
        part = jnp.dot(act, wd_ref[sl, :], preferred_element_type=F32)
        if j == 0:
            acc_ref[...] = part
        else:
            acc_ref[...] += part
    y = x + 0.5 * acc_ref[...]
    if final:
        y = _rms(y, fin_ref[...])
    o_ref[...] = y


def _ffn(x, norm_g, wg, wu, wd, final_g, *, final, tm):
    n = x.shape[0]
    row = pl.BlockSpec((tm, D_MODEL), lambda i: (i, 0))
    return pl.pallas_call(
        functools.partial(_ffn_kernel, final=final),
        grid=(n // tm,),
        in_specs=[row, _resident((1, D_MODEL)), _resident(wg.shape), _resident(wu.shape), _resident(wd.shape),
                  _resident((1, D_MODEL))],
        out_specs=row,
        out_shape=jax.ShapeDtypeStruct((n, D_MODEL), F32),
        scratch_shapes=[pltpu.VMEM((tm, D_MODEL), F32)],
        compiler_params=_params(("parallel",)),
        name="ffn",
    )(x, norm_g, wg, wu, wd, final_g)


def _rope(x, cos, sin_signed, first_half):
    fwd = pltpu.roll(x, LANES - HEAD_DIM // 2, axis=1)
    bwd = pltpu.roll(x, HEAD_DIM // 2, axis=1)
    return x * cos + jnp.where(first_half, fwd, bwd) * sin_signed


def _proj_kernel(x_ref, g_ref, w_ref, gb_ref, cos_ref, sin_ref, qkv_ref, u_ref, gates_ref):
    xn = _rms(x_ref[...], g_ref[...]).astype(BF16)
    cos = cos_ref[...]
    sin_signed = sin_ref[...]
    lane = lax.broadcasted_iota(jnp.int32, cos.shape, 1)
    first_half = jnp.bitwise_and(lane, HEAD_DIM // 2) == 0
    for c in range(N_ATT_COLS // LANES):
        sl = slice(c * LANES, (c + 1) * LANES)
        p = jnp.dot(xn, w_ref[:, sl], preferred_element_type=F32)
        is_v = (c // (ATT_W // LANES)) % 3 == 2
        qkv_ref[:, sl] = p if is_v else _rope(p, cos, sin_signed, first_half)
    for c in range(C_CONV // 256):
        a = jnp.dot(xn, w_ref[:, N_ATT_COLS + c * 256:N_ATT_COLS + (c + 1) * 256], preferred_element_type=F32)
        b = jnp.dot(xn, w_ref[:, N_ATT_COLS + C_CONV + c * 256:N_ATT_COLS + C_CONV + (c + 1) * 256],
                    preferred_element_type=F32)
        u_ref[:, c * 256:(c + 1) * 256] = a * jax.nn.sigmoid(b)
    base = N_ATT_COLS + 2 * C_CONV
    for c in range(2 * D_MODEL // 256):
        gl = jnp.dot(xn, w_ref[:, base + c * 256:base + (c + 1) * 256], preferred_element_type=F32)
        gates_ref[:, c * 256:(c + 1) * 256] = jax.nn.sigmoid(gl + gb_ref[:, c * 256:(c + 1) * 256]).astype(BF16)


def _proj(x, norm_g, w_in, gate_bias, cos, sin_signed, *, tm):
    n = x.shape[0]
    tiles_per_seq = cos.shape[0] // tm
    row = lambda w: pl.BlockSpec((tm, w), lambda i: (i, 0))
    tab = pl.BlockSpec((tm, LANES), lambda i: (i % tiles_per_seq, 0))
    return pl.pallas_call(
        _proj_kernel,
        grid=(n // tm,),
        in_specs=[row(D_MODEL), _resident((1, D_MODEL)), _resident(w_in.shape), _resident((1, 2 * D_MODEL)), tab, tab],
        out_specs=[row(N_ATT_COLS), row(C_CONV), row(2 * D_MODEL)],
        out_shape=[jax.ShapeDtypeStruct((n, N_ATT_COLS), F32), jax.ShapeDtypeStruct((n, C_CONV), F32),
                   jax.ShapeDtypeStruct((n, 2 * D_MODEL), BF16)],
        compiler_params=_params(("parallel",)),
        name="proj",
    )(x, norm_g, w_in, gate_bias, cos, sin_signed)


def _rope_tables(positions):
    half = HEAD_DIM // 2
    inv_freq = ROPE_THETA ** (-jnp.arange(half, dtype=F32) * 2.0 / HEAD_DIM)
    ang = positions.astype(F32)[:, None] * inv_freq[None, :]
    cos = jnp.tile(jnp.cos(ang), (1, LANES // half))
    sin = jnp.tile(jnp.concatenate([-jnp.sin(ang), jnp.sin(ang)], axis=1), (1, LANES // HEAD_DIM))
    return cos, sin


def _softmax_pv(s, v2):
    m = jnp.max(s, axis=-1, keepdims=True)
    p = jnp.exp(s - m)
    l = jnp.sum(p, axis=-1, keepdims=True)
    pv = jnp.dot(p.astype(BF16), v2, preferred_element_type=F32)
    return pv * (1.0 / l), m + jnp.log(l)


def _attn_prompt_kernel(bias_ref, q_ref, kp_ref, kc_ref, vp_ref, vc_ref, o_ref, lse_ref, *, dil):
    bias = bias_ref[...]
    lane = lax.broadcasted_iota(jnp.int32, (BLOCK, LANES), 1)
    low = lane < HEAD_DIM

    def one_class(r, carry):
        rows = pl.ds(r, BLOCK, stride=dil) if dil > 1 else pl.ds(0, BLOCK)
        q = q_ref[0, rows, :] * SCALE
        k2 = jnp.concatenate([kp_ref[0, rows, :], kc_ref[0, rows, :]], axis=0).astype(BF16)
        v2 = jnp.concatenate([vp_ref[0, rows, :], vc_ref[0, rows, :]], axis=0).astype(BF16)
        outs = []
        for keep in (low, jnp.logical_not(low)):
            qm = jnp.where(keep, q, 0.0).astype(BF16)
            s = lax.dot_general(qm, k2, (((1,), (1,)), ((), ())), preferred_element_type=F32) + bias
            outs.append(_softmax_pv(s, v2))
        o_ref[0, rows, :] = jnp.where(low, outs[0][0], outs[1][0])
        lse_ref[0, rows, :] = jnp.where(low, outs[0][1], outs[1][1])
        return carry

    if dil > 1:
        lax.fori_loop(0, dil, one_class, 0)
    else:
        one_class(0, 0)


def _attn_prompt(qkv, g, dil):
    b, s, _ = qkv.shape
    sb = BLOCK * dil
    pairs = ATT_W // LANES
    qi = np.arange(BLOCK)[:, None]
    kj = np.arange(2 * BLOCK)[None, :]
    dist = qi + BLOCK - kj
    ok = (dist >= 0) & (dist <= BLOCK)
    bias = np.stack([np.where(ok & (kj >= BLOCK), 0.0, -np.inf), np.where(ok, 0.0, -np.inf)]).astype(np.float32)

    def col(which):
        return lambda bi, c, p: (bi, c, (3 * g + which) * pairs + p)

    def col_prev(which):
        return lambda bi, c, p: (bi, jnp.maximum(c - 1, 0), (3 * g + which) * pairs + p)

    blk = (1, sb, LANES)
    out_spec = pl.BlockSpec(blk, lambda bi, c, p: (bi, c, p))
    return pl.pallas_call(
        functools.partial(_attn_prompt_kernel, dil=dil),
        grid=(b, s // sb, pairs),
        in_specs=[pl.BlockSpec((None, BLOCK, 2 * BLOCK), lambda bi, c, p: (jnp.minimum(c, 1), 0, 0)),
                  pl.BlockSpec(blk, col(0)), pl.BlockSpec(blk, col_prev(1)), pl.BlockSpec(blk, col(1)),
                  pl.BlockSpec(blk, col_prev(2)), pl.BlockSpec(blk, col(2))],
        out_specs=[out_spec, out_spec],
        out_shape=[jax.ShapeDtypeStruct((b, s, ATT_W), F32)] * 2,
        compiler_params=_params(("parallel", "parallel", "parallel")),
        name="attn_prompt_d%d" % dil,
    )(jnp.asarray(bias), qkv, qkv, qkv, qkv, qkv)


def _attn_sample_kernel(cache_ref, q_ref, kn_ref, vn_ref, cache_out_ref, o_ref, lse_ref, *, dil, width, t_new):
    tq = q_ref.shape[1]
    qrow = lax.broadcasted_iota(jnp.int32, (tq, width), 0)
    kcol = lax.broadcasted_iota(jnp.int32, (tq, width), 1)
    dist = width + qrow - kcol
    ok_cache = (jnp.bitwise_and(dist, dil - 1) == 0) & (dist <= BLOCK * dil)
    lane = lax.broadcasted_iota(jnp.int32, (tq, LANES), 1)
    low = lane < HEAD_DIM
    trow = lax.broadcasted_iota(jnp.int32, (tq, 1), 0)
    q_all = q_ref[0] * SCALE
    for p in range(ATT_W // LANES):
        sl = slice(p * LANES, (p + 1) * LANES)
        kt = cache_ref[0, 0, 2 * p:2 * p + 2].reshape(LANES, width)
        vt = cache_ref[0, 1, 2 * p:2 * p + 2].reshape(LANES, width)
        cache_out_ref[0, 0, 2 * p:2 * p + 2] = pltpu.roll(kt, width - t_new, axis=1).reshape(2, HEAD_DIM, width)
        cache_out_ref[0, 1, 2 * p:2 * p + 2] = pltpu.roll(vt, width - t_new, axis=1).reshape(2, HEAD_DIM, width)
        ktb = kt.astype(BF16)
        vtb = vt.astype(BF16)
        q = q_all[:, sl]
        kn = kn_ref[0, :, sl].astype(BF16).astype(F32)
        vn = vn_ref[0, :, sl].astype(BF16).astype(F32)
        outs = []
        for keep in (low, jnp.logical_not(low)):
            qm = jnp.where(keep, q, 0.0).astype(BF16)
            s = jnp.where(ok_cache, jnp.dot(qm, ktb, preferred_element_type=F32), -jnp.inf)
            qf = qm.astype(F32)
            s_new = []
            for t in range(t_new):
                sn = jnp.sum(qf * kn[t:t + 1, :], axis=-1, keepdims=True)
                ok_new = (trow >= t) & (jnp.bitwise_and(trow - t, dil - 1) == 0)
                s_new.append(jnp.where(ok_new, sn, -jnp.inf))
            m = jnp.max(s, axis=-1, keepdims=True)
            for sn in s_new:
                m = jnp.maximum(m, sn)
            pc = jnp.exp(s - m)
            l = jnp.sum(pc, axis=-1, keepdims=True)
            acc = lax.dot_general(pc.astype(BF16), vtb, (((1,), (1,)), ((), ())), preferred_element_type=F32)
            for t, sn in enumerate(s_new):
                pn = jnp.exp(sn - m)
                l = l + pn
                acc = acc + pn.astype(BF16).astype(F32) * vn[t:t + 1, :]
            outs.append((acc * (1.0 / l), m + jnp.log(l)))
        o_ref[0, :, sl] = jnp.where(low, outs[0][0], outs[1][0])
        lse_ref[0, :, sl] = jnp.where(low, outs[0][1], outs[1][1])


def _attn_sample(cache_t, qkv_s, g, dil, t_new):
    db, _, _, _, width = cache_t.shape
    tq = qkv_s.shape[1]
    assert dil & (dil - 1) == 0, "masks use power-of-two dilation"
    cblk = pl.BlockSpec((1, 2, H_G, HEAD_DIM, width), lambda bi: (bi, 0, 0, 0, 0))
    new = lambda which: pl.BlockSpec((1, tq, ATT_W), lambda bi: (bi, 0, 3 * g + which))
    oblk = pl.BlockSpec((1, tq, ATT_W), lambda bi: (bi, 0, 0))
    return pl.pallas_call(
        functools.partial(_attn_sample_kernel, dil=dil, width=width, t_new=t_new),
        grid=(db,),
        in_specs=[cblk, new(0), new(1), new(2)],
        out_specs=[cblk, oblk, oblk],
        out_shape=[jax.ShapeDtypeStruct(cache_t.shape, F32), jax.ShapeDtypeStruct((db, tq, ATT_W), F32),
                   jax.ShapeDtypeStruct((db, tq, ATT_W), F32)],
        compiler_params=_params(("parallel",)),
        name="attn_sample_d%d" % dil,
    )(cache_t, qkv_s, qkv_s, qkv_s)


def _mixer_tail(y1, outs, lses, conv, gates, cb, lng, lnb, w_conv_out, w_att_out, w_o):
    m = jnp.maximum(jnp.maximum(lses[0], lses[1]), lses[2])
    es = [jnp.exp(l - m) for l in lses]
    inv = 1.0 / (es[0] + es[1] + es[2])
    att = (es[0] * inv) * outs[0] + (es[1] * inv) * outs[1] + (es[2] * inv) * outs[2]
    a = jnp.dot(att.astype(BF16), w_att_out, preferred_element_type=F32)
    c = conv + cb
    mu = jnp.mean(c, axis=-1, keepdims=True)
    xc = c - mu
    cn = xc * lax.rsqrt(jnp.mean(xc * xc, axis=-1, keepdims=True) + LN_EPS) * lng + lnb
    cs = (cn * jax.nn.sigmoid(cn)).astype(BF16)
    cbr = jnp.dot(cs, w_conv_out, preferred_element_type=F32)
    ga = gates[:, :D_MODEL].astype(F32)
    gc = gates[:, D_MODEL:].astype(F32)
    merged = (ga * a + gc * cbr).astype(BF16)
    return y1 + jnp.dot(merged, w_o, preferred_element_type=F32)


CONV_ROWS = 64
CONV_LANES = 256


def _tail_prompt_kernel(y1_ref, o0, o1, o2, l0, l1, l2, u_ref, halo_ref, gates_ref, cw_ref, cb_ref, lng_ref, lnb_ref,
                        wco_ref, wao_ref, wo_ref, out_ref, uf_ref, conv_ref, *, tm):
    first = pl.program_id(1) == 0
    uf_ref[0:HALO, :] = jnp.where(first, 0.0, halo_ref[0])
    uf_ref[HALO:HALO + tm, :] = u_ref[0]
    off = HALO - (CONV_WIDTH - 1)
    for lc in range(C_CONV // CONV_LANES):
        ls = slice(lc * CONV_LANES, (lc + 1) * CONV_LANES)
        for rc in range(tm // CONV_ROWS):
            acc = jnp.zeros((CONV_ROWS, CONV_LANES), F32)
            for j in range(CONV_WIDTH):
                acc = acc + cw_ref[j:j + 1, ls] * uf_ref[pl.ds(rc * CONV_ROWS + off + j, CONV_ROWS), ls]
            conv_ref[rc * CONV_ROWS:(rc + 1) * CONV_ROWS, ls] = acc
    out_ref[0] = _mixer_tail(y1_ref[0], (o0[0], o1[0], o2[0]), (l0[0], l1[0], l2[0]), conv_ref[...], gates_ref[0],
                             cb_ref[...], lng_ref[...], lnb_ref[...], wco_ref[...], wao_ref[...], wo_ref[...])


def _tail_prompt(y1, outs, lses, u, gates, cw, cb, lng, lnb, wco, wao, wo, *, tm):
    b, s, _ = y1.shape
    row = lambda w: pl.BlockSpec((1, tm, w), lambda bi, t: (bi, t, 0))
    halo = pl.BlockSpec((1, HALO, C_CONV), lambda bi, t: (bi, jnp.maximum(t * (tm // HALO) - 1, 0), 0))
    return pl.pallas_call(
        functools.partial(_tail_prompt_kernel, tm=tm),
        grid=(b, s // tm),
        in_specs=[row(D_MODEL)] + [row(ATT_W)] * 6 + [row(C_CONV), halo, row(2 * D_MODEL), _resident(cw.shape),
                  _resident(cb.shape), _resident(lng.shape), _resident(lnb.shape), _resident(wco.shape),
                  _resident(wao.shape), _resident(wo.shape)],
        out_specs=row(D_MODEL),
        out_shape=jax.ShapeDtypeStruct((b, s, D_MODEL), F32),
        scratch_shapes=[pltpu.VMEM((HALO + tm, C_CONV), F32), pltpu.VMEM((tm, C_CONV), F32)],
        compiler_params=_params(("parallel", "arbitrary")),
        name="tail_prompt",
    )(y1, *outs, *lses, u, u, gates, cw, cb, lng, lnb, wco, wao, wo)


def _conv_sample_kernel(u_ref, st_ref, cw_ref, conv_ref, st_out_ref, *, db, t_new):
    hist = CONV_WIDTH - 1

    def slab(i):
        return st_ref[i] if i < hist else u_ref[(i - hist) * db:(i - hist + 1) * db, :]

    for t in range(t_new):
        acc = jnp.zeros((db, CONV_LANES), F32)
        for j in range(CONV_WIDTH):
            acc = acc + cw_ref[j:j + 1, :] * slab(t + j)
        conv_ref[t * db:(t + 1) * db, :] = acc
    for i in range(hist):
        st_out_ref[i] = slab(i + t_new)


def _conv_sample(u, state_t, cw, *, db, t_new):
    hist = CONV_WIDTH - 1
    st = pl.BlockSpec((hist, db, CONV_LANES), lambda c: (0, 0, c))
    col = lambda rows: pl.BlockSpec((rows, CONV_LANES), lambda c: (0, c))
    return pl.pallas_call(
        functools.partial(_conv_sample_kernel, db=db, t_new=t_new),
        grid=(C_CONV // CONV_LANES,),
        in_specs=[col(t_new * db), st, col(cw.shape[0])],
        out_specs=[col(t_new * db), st],
        out_shape=[jax.ShapeDtypeStruct(u.shape, F32), jax.ShapeDtypeStruct(state_t.shape, F32)],
        compiler_params=_params(("parallel",)),
        name="conv_sample",
    )(u, state_t, cw)


def _tail_sample_kernel(y1_ref, o0, o1, o2, l0, l1, l2, conv_ref, gates_ref, cb_ref, lng_ref, lnb_ref,
                        wco_ref, wao_ref, wo_ref, out_ref):
    out_ref[...] = _mixer_tail(y1_ref[...], (o0[...], o1[...], o2[...]), (l0[...], l1[...], l2[...]), conv_ref[...],
                               gates_ref[...], cb_ref[...], lng_ref[...], lnb_ref[...], wco_ref[...], wao_ref[...],
                               wo_ref[...])


def _tail_sample(y1, outs, lses, conv, gates, cb, lng, lnb, wco, wao, wo, *, tm):
    n = y1.shape[0]
    row = lambda w: pl.BlockSpec((tm, w), lambda i: (i, 0))
    return pl.pallas_call(
        _tail_sample_kernel,
        grid=(n // tm,),
        in_specs=[row(D_MODEL)] + [row(ATT_W)] * 6 + [row(C_CONV), row(2 * D_MODEL), _resident(cb.shape),
                  _resident(lng.shape), _resident(lnb.shape), _resident(wco.shape), _resident(wao.shape),
                  _resident(wo.shape)],
        out_specs=row(D_MODEL),
        out_shape=jax.ShapeDtypeStruct((n, D_MODEL), F32),
        compiler_params=_params(("parallel",)),
        name="tail_sample",
    )(y1, *outs, *lses, conv, gates, cb, lng, lnb, wco, wao, wo)


def kernel(x_prompt, x_sample, cache_kv_w128, cache_kv_w512, cache_kv_w2048, state_conv, ffn1_norm, ffn1_w_gate,
           ffn1_w_up, ffn1_w_down, mix_norm, w_in, gate_bias, conv_w, conv_b, conv_ln_g, conv_ln_b, w_conv_out,
           w_att_out, w_o, ffn2_norm, ffn2_w_gate, ffn2_w_up, ffn2_w_down, final_norm):
    assert ffn1_norm.shape[0] == 1, "single layer"
    b, s, _ = x_prompt.shape
    db, t_new, _ = x_sample.shape
    caches = (cache_kv_w128, cache_kv_w512, cache_kv_w2048)
    bf = lambda w: w[0].astype(BF16)
    ffn1 = (ffn1_norm, bf(ffn1_w_gate), bf(ffn1_w_up), bf(ffn1_w_down))
    ffn2 = (ffn2_norm, bf(ffn2_w_gate), bf(ffn2_w_up), bf(ffn2_w_down))
    w_in_b, wco, wao, wo = bf(w_in), bf(w_conv_out), bf(w_att_out), bf(w_o)
    cw = jnp.pad(conv_w[0], ((0, HALO - CONV_WIDTH), (0, 0)))
    fin = final_norm[None, :]

    xp = x_prompt.reshape(b * s, D_MODEL)
    y1p = _ffn(xp, *ffn1, fin, final=False, tm=512)
    cos_p, sin_p = _rope_tables(jnp.arange(s, dtype=jnp.int32))
    qkv_p, u_p, gates_p = _proj(y1p, mix_norm, w_in_b, gate_bias, cos_p, sin_p, tm=256)
    qkv_p3 = qkv_p.reshape(b, s, N_ATT_COLS)
    outs, lses = [], []
    for g, (_, dil) in enumerate(GROUPS):
        o, l = _attn_prompt(qkv_p3, g, dil)
        outs.append(o)
        lses.append(l)
    u_p3 = u_p.reshape(b, s, C_CONV)
    y2p = _tail_prompt(y1p.reshape(b, s, D_MODEL), outs, lses, u_p3, gates_p.reshape(b, s, 2 * D_MODEL), cw, conv_b,
                       conv_ln_g, conv_ln_b, wco, wao, wo, tm=256)
    y_prompt = _ffn(y2p.reshape(b * s, D_MODEL), *ffn2, fin, final=True, tm=512).reshape(b, s, D_MODEL)
    kv_prompt = []
    for g, (win, _) in enumerate(GROUPS):
        keep = min(win, s)
        kv = qkv_p3[:, s - keep:, (3 * g + 1) * ATT_W:(3 * g + 3) * ATT_W]
        kv_prompt.append(kv.reshape(1, b, keep, 2, H_G, HEAD_DIM))
    conv_prompt = u_p3[None, :, s - (CONV_WIDTH - 1):, :]

    n_s = db * t_new
    xs = jnp.swapaxes(x_sample, 0, 1).reshape(n_s, D_MODEL)
    y1s = _ffn(xs, *ffn1, fin, final=False, tm=n_s)
    pos_s = PAST_LEN + jnp.repeat(jnp.arange(t_new, dtype=jnp.int32), db)
    cos_s, sin_s = _rope_tables(pos_s)
    qkv_s, u_s, gates_s = _proj(y1s, mix_norm, w_in_b, gate_bias, cos_s, sin_s, tm=n_s // 2)
    qkv_sb = jnp.swapaxes(qkv_s.reshape(t_new, db, N_ATT_COLS), 0, 1)
    qkv_pad = jnp.pad(qkv_sb, ((0, 0), (0, SAMPLE_Q_ROWS - t_new), (0, 0)))
    outs, lses, kv_sample = [], [], []
    for g, (win, dil) in enumerate(GROUPS):
        cache_t = jnp.transpose(caches[g][0], (0, 2, 3, 4, 1))
        shifted, o, l = _attn_sample(cache_t, qkv_pad, g, dil, t_new)
        new_kv = qkv_sb[:, :, (3 * g + 1) * ATT_W:(3 * g + 3) * ATT_W].reshape(db, t_new, 2, H_G, HEAD_DIM)
        shifted = lax.dynamic_update_slice(shifted, jnp.transpose(new_kv, (0, 2, 3, 4, 1)),
                                           (0, 0, 0, 0, cache_t.shape[-1] - t_new))
        kv_sample.append(jnp.transpose(shifted, (0, 4, 1, 2, 3))[None])
        outs.append(jnp.swapaxes(o[:, :t_new], 0, 1).reshape(n_s, ATT_W))
        lses.append(jnp.swapaxes(l[:, :t_new], 0, 1).reshape(n_s, ATT_W))
    state_t = jnp.swapaxes(state_conv[0], 0, 1)
    conv_s, state_new = _conv_sample(u_s, state_t, cw, db=db, t_new=t_new)
    y2s = _tail_sample(y1s, outs, lses, conv_s, gates_s, conv_b, conv_ln_g, conv_ln_b, wco, wao, wo, tm=n_s // 2)
    y_s = _ffn(y2s, *ffn2, fin, final=True, tm=n_s)
    y_sample = jnp.swapaxes(y_s.reshape(t_new, db, D_MODEL), 0, 1)
    conv_sample = jnp.swapaxes(state_new, 0, 1)[None]

    return (y_prompt, y_sample, kv_prompt[0], kv_prompt[1], kv_prompt[2], conv_prompt,
            kv_sample[0], kv_sample[1], kv_sample[2], conv_sample)
```

```python
import functools

import numpy as np
import jax
import jax.numpy as jnp
from jax import lax
from jax.experimental import pallas as pl
from jax.experimental.pallas import tpu as pltpu

D_MODEL = 1024
D_FF = 2816
GROUPS = ((128, 1), (512, 4), (2048, 16))
N_GROUPS = len(GROUPS)
H_G = 8
HEAD_DIM = 64
ATT_W = H_G * HEAD_DIM
BLOCK = 128
C_CONV = D_MODEL
CONV_WIDTH = 31
N_ATT_COLS = N_GROUPS * 3 * ATT_W
N_IN = N_ATT_COLS + 2 * C_CONV + 2 * D_MODEL
ROPE_THETA = 10000.0
PAST_LEN = 8192
RMS_EPS = 1e-6
LN_EPS = 1e-5
SCALE = HEAD_DIM ** -0.5

LANES = 128
SUBLANES = 8
HALO = 32
SAMPLE_Q_ROWS = 8
PAIRS = ATT_W // LANES
CHUNKS = 3 * PAIRS
SUPER = BLOCK * max(d for _, d in GROUPS)
VMEM_LIMIT = 56 * 1024 * 1024

F32 = jnp.float32
BF16 = jnp.bfloat16
NT = (((1,), (1,)), ((), ()))


def _params(sem):
    return pltpu.CompilerParams(dimension_semantics=sem, vmem_limit_bytes=VMEM_LIMIT)


def _resident(shape):
    nd = len(shape)
    return pl.BlockSpec(shape, lambda *_: (0,) * nd, pipeline_mode=pl.Buffered(1))


def _rms(x, g):
    return x * lax.rsqrt(jnp.mean(x * x, axis=-1, keepdims=True) + RMS_EPS) * g


FF_CHUNK = 256


def _ffn_kernel(x_ref, g_ref, wg_ref, wu_ref, wd_ref, fin_ref, o_ref, acc_ref, *, final):
    x = x_ref[...]
    h = _rms(x, g_ref[...]).astype(BF16)
    for j in range(D_FF // FF_CHUNK):
        sl = slice(j * FF_CHUNK, (j + 1) * FF_CHUNK)
        gate = jnp.dot(h, wg_ref[:, sl], preferred_element_type=F32)
        up = jnp.dot(h, wu_ref[:, sl], preferred_element_type=F32)
        act = (gate * jax.nn.sigmoid(gate) * up).astype(BF16)
        part = jnp.dot(act, wd_ref[sl, :], preferred_element_type=F32)
        if j == 0:
            acc_ref[...] = part
        else:
            acc_ref[...] += part
    y = x + 0.5 * acc_ref[...]
    if final:
        y = _rms(y, fin_ref[...])
    o_ref[...] = y


def _ffn(x, norm_g, wg, wu, wd, final_g, *, final, tm):
    n = x.shape[0]
    row = pl.BlockSpec((tm, D_MODEL), lambda i: (i, 0))
    return pl.pallas_call(
        functools.partial(_ffn_kernel, final=final),
        grid=(n // tm,),
        in_specs=[row, _resident((1, D_MODEL)), _resident(wg.shape), _resident(wu.shape), _resident(wd.shape),
                  _resident((1, D_MODEL))],
        out_specs=row,
        out_shape=jax.ShapeDtypeStruct((n, D_MODEL), F32),
        scratch_shapes=[pltpu.VMEM((tm, D_MODEL), F32)],
        compiler_params=_params(("parallel",)),
        name="ffn",
    )(x, norm_g, wg, wu, wd, final_g)


def _rope(x, cos, sin_signed, first_half):
    fwd = pltpu.roll(x, LANES - HEAD_DIM // 2, axis=1)
    bwd = pltpu.roll(x, HEAD_DIM // 2, axis=1)
    return x * cos + jnp.where(first_half, fwd, bwd) * sin_signed


def _rope_tables(positions):
    half = HEAD_DIM // 2
    inv_freq = ROPE_THETA ** (-jnp.arange(half, dtype=F32) * 2.0 / HEAD_DIM)
    ang = positions.astype(F32)[:, None] * inv_freq[None, :]
    cos = jnp.tile(jnp.cos(ang), (1, LANES // half))
    sin = jnp.tile(jnp.concatenate([-jnp.sin(ang), jnp.sin(ang)], axis=1), (1, LANES // HEAD_DIM))
    return cos, sin


def _proj_att_kernel(x_ref, g_ref, w_ref, cos_ref, sin_ref, q0_ref, q1_ref, q2_ref, t0_ref, t1_ref, t2_ref,
                     stage_ref, *, tm):
    xn = _rms(x_ref[0], g_ref[...]).astype(BF16)
    cos = cos_ref[...]
    sin_signed = sin_ref[...]
    lane = lax.broadcasted_iota(jnp.int32, cos.shape, 1)
    first_half = jnp.bitwise_and(lane, HEAD_DIM // 2) == 0
    outs = (q0_ref, q1_ref, q2_ref)
    tails = (t0_ref, t1_ref, t2_ref)
    for c in range(N_GROUPS * CHUNKS):
        g, lc = divmod(c, CHUNKS)
        which = lc // PAIRS
        dil = GROUPS[g][1]
        if c % 2 == 0:
            wide = jnp.dot(xn, w_ref[:, c * LANES:(c + 2) * LANES], preferred_element_type=F32)
        p = wide[:, (c % 2) * LANES:(c % 2 + 1) * LANES]
        if which < 2:
            p = _rope(p, cos, sin_signed, first_half)
        if which == 0:
            p = p * SCALE
        else:
            rows = tails[g].shape[1]
            tails[g][0, :, (lc - PAIRS) * LANES:(lc - PAIRS + 1) * LANES] = p[tm - rows:, :]
        if dil == 1:
            outs[g][0, lc, 0] = p.astype(BF16)
        else:
            stage_ref[...] = p
            for r in range(dil):
                outs[g][0, lc, r] = stage_ref[pl.ds(r, tm // dil, stride=dil), :].astype(BF16)


def _proj_att(y1, norm_g, w_att, cos, sin_signed, *, tm):
    b, s, _ = y1.shape
    tiles = s // tm
    tab = pl.BlockSpec((tm, LANES), lambda bi, t: (t, 0))
    in_specs = [pl.BlockSpec((1, tm, D_MODEL), lambda bi, t: (bi, t, 0)), _resident((1, D_MODEL)),
                _resident(w_att.shape), tab, tab]
    out_specs, out_shape = [], []
    for _, dil in GROUPS:
        out_specs.append(pl.BlockSpec((1, CHUNKS, dil, tm // dil, LANES), lambda bi, t: (bi, 0, 0, t, 0)))
        out_shape.append(jax.ShapeDtypeStruct((b, CHUNKS, dil, s // dil, LANES), BF16))
    for win, _ in GROUPS:
        keep = min(win, s)
        rows = min(tm, keep)
        first_tile = tiles - keep // rows
        out_specs.append(pl.BlockSpec((1, rows, 2 * ATT_W),
                                      lambda bi, t, ft=first_tile: (bi, jnp.maximum(t - ft, 0), 0)))
        out_shape.append(jax.ShapeDtypeStruct((b, keep, 2 * ATT_W), F32))
    return pl.pallas_call(
        functools.partial(_proj_att_kernel, tm=tm),
        grid=(b, tiles),
        in_specs=in_specs,
        out_specs=out_specs,
        out_shape=out_shape,
        scratch_shapes=[pltpu.VMEM((tm, LANES), F32)],
        compiler_params=_params(("parallel", "arbitrary")),
        name="proj_att",
    )(y1, norm_g, w_att, cos, sin_signed)


CONV_ROWS = 64


def _proj_conv_kernel(x_ref, g_ref, w_ref, gb_ref, cw_ref, c_ref, gates_ref, utail_ref, uf_ref, sh_ref, *, tm):
    t = pl.program_id(1)
    xn = _rms(x_ref[0], g_ref[...]).astype(BF16)

    @pl.when(t == 0)
    def _():
        uf_ref[0:HALO, :] = jnp.zeros((HALO, C_CONV), F32)

    @pl.when(t > 0)
    def _():
        uf_ref[0:HALO, :] = uf_ref[tm:tm + HALO, :]

    for c in range(C_CONV // 256):
        a = jnp.dot(xn, w_ref[:, c * 256:(c + 1) * 256], preferred_element_type=F32)
        b = jnp.dot(xn, w_ref[:, C_CONV + c * 256:C_CONV + (c + 1) * 256], preferred_element_type=F32)
        uf_ref[HALO:HALO + tm, c * 256:(c + 1) * 256] = a * jax.nn.sigmoid(b)
    utail_ref[0] = uf_ref[tm:tm + HALO, :]

    off = HALO - (CONV_WIDTH - 1)
    sh_rows = sh_ref.shape[1]
    for lc in range(C_CONV // LANES):
        ls = slice(lc * LANES, (lc + 1) * LANES)
        for ph in range(1, SUBLANES):
            sh_ref[ph - 1] = uf_ref[pl.ds(ph, sh_rows), ls]
        for rc in range(tm // CONV_ROWS):
            acc = jnp.zeros((CONV_ROWS, LANES), F32)
            for j in range(CONV_WIDTH):
                a8, ph = divmod(off + j, SUBLANES)
                r0 = rc * CONV_ROWS + a8 * SUBLANES
                slab = uf_ref[r0:r0 + CONV_ROWS, ls] if ph == 0 else sh_ref[ph - 1, r0:r0 + CONV_ROWS, :]
                acc = acc + cw_ref[j:j + 1, ls] * slab
            c_ref[0, rc * CONV_ROWS:(rc + 1) * CONV_ROWS, ls] = acc

    base = 2 * C_CONV
    for c in range(2 * D_MODEL // 256):
        gl = jnp.dot(xn, w_ref[:, base + c * 256:base + (c + 1) * 256], preferred_element_type=F32)
        gates_ref[0, :, c * 256:(c + 1) * 256] = jax.nn.sigmoid(gl + gb_ref[:, c * 256:(c + 1) * 256]).astype(BF16)


def _proj_conv(y1, norm_g, w_cg, gate_bias, cw, *, tm):
    b, s, _ = y1.shape
    row = lambda w: pl.BlockSpec((1, tm, w), lambda bi, t: (bi, t, 0))
    sh_rows = tm + HALO - SUBLANES
    return pl.pallas_call(
        functools.partial(_proj_conv_kernel, tm=tm),
        grid=(b, s // tm),
        in_specs=[row(D_MODEL), _resident((1, D_MODEL)), _resident(w_cg.shape), _resident((1, 2 * D_MODEL)),
                  _resident(cw.shape)],
        out_specs=[row(C_CONV), row(2 * D_MODEL), pl.BlockSpec((1, HALO, C_CONV), lambda bi, t: (bi, 0, 0))],
        out_shape=[jax.ShapeDtypeStruct((b, s, C_CONV), F32), jax.ShapeDtypeStruct((b, s, 2 * D_MODEL), BF16),
                   jax.ShapeDtypeStruct((b, HALO, C_CONV), F32)],
        scratch_shapes=[pltpu.VMEM((tm + HALO, C_CONV), F32), pltpu.VMEM((SUBLANES - 1, sh_rows, LANES), F32)],
        compiler_params=_params(("parallel", "arbitrary")),
        name="proj_conv",
    )(y1, norm_g, w_cg, gate_bias, cw)


def _attn_fused_kernel(bias_ref, q3, k3c, k3p, v3c, v3p, q2, k2c, k2p, v2c, v2p, q1, k1c, k1p, v1c, v1p,
                       o_ref, acc_ref, m_ref, l_ref):
    first = pl.program_id(1) == 0
    bias_in = bias_ref[1]
    bias_first = jnp.where(first, bias_ref[0], bias_in)
    lane = lax.broadcasted_iota(jnp.int32, (BLOCK, LANES), 1)
    low = lane < HEAD_DIM

    def partials(q, kk, vv, bias):
        res = []
        for keep in (low, jnp.logical_not(low)):
            qm = jnp.where(keep, q, jnp.zeros_like(q))
            s = lax.dot_general(qm, kk, NT, preferred_element_type=F32) + bias
            m = jnp.max(s, axis=-1, keepdims=True)
            p = jnp.exp(s - m)
            l = jnp.sum(p, axis=-1, keepdims=True)
            pv = jnp.dot(p.astype(BF16), vv, preferred_element_type=F32)
            res.append((m, l, pv))
        return tuple(jnp.where(low, a, b) for a, b in zip(res[0], res[1]))

    def init(rows, part):
        m_g, l_g, pv_g = part
        m_ref[rows, :] = m_g
        l_ref[rows, :] = l_g
        acc_ref[rows, :] = pv_g

    def merge(rows, part, final):
        m_g, l_g, pv_g = part
        m_o = m_ref[rows, :]
        m_n = jnp.maximum(m_o, m_g)
        a = jnp.exp(m_o - m_n)
        b = jnp.exp(m_g - m_n)
        l_n = a * l_ref[rows, :] + b * l_g
        acc_n = a * acc_ref[rows, :] + b * pv_g
        if final:
            o_ref[0, rows, :] = (acc_n * (1.0 / l_n)).astype(BF16)
        else:
            m_ref[rows, :] = m_n
            l_ref[rows, :] = l_n
            acc_ref[rows, :] = acc_n

    def cat(prev, cur):
        return jnp.concatenate([prev, cur], axis=0)

    d3 = GROUPS[2][1]
    per3 = 4

    def body3(i, carry):
        for rr in range(per3):
            r = i * per3 + rr
            part = partials(q3[0, 0, r], cat(k3p[0, 0, r], k3c[0, 0, r]), cat(v3p[0, 0, r], v3c[0, 0, r]), bias_first)
            init(pl.ds(r, BLOCK, stride=d3), part)
        return carry

    lax.fori_loop(0, d3 // per3, body3, 0)

    d2 = GROUPS[1][1]
    nb2 = SUPER // d2 // BLOCK
    for r in range(d2):
        part = partials(q2[0, 0, r, 0:BLOCK], cat(k2p[0, 0, r], k2c[0, 0, r, 0:BLOCK]),
                        cat(v2p[0, 0, r], v2c[0, 0, r, 0:BLOCK]), bias_first)
        merge(pl.ds(r, BLOCK, stride=d2), part, False)

    def body2(c, carry):
        lo = pl.multiple_of((c - 1) * BLOCK, BLOCK)
        mid = pl.multiple_of(c * BLOCK, BLOCK)
        for r in range(d2):
            part = partials(q2[0, 0, r, pl.ds(mid, BLOCK)], k2c[0, 0, r, pl.ds(lo, 2 * BLOCK)],
                            v2c[0, 0, r, pl.ds(lo, 2 * BLOCK)], bias_in)
            merge(pl.ds(c * (BLOCK * d2) + r, BLOCK, stride=d2), part, False)
        return carry

    lax.fori_loop(1, nb2, body2, 0)

    nb1 = SUPER // BLOCK
    per1 = 3
    part = partials(q1[0, 0, 0, 0:BLOCK], cat(k1p[0, 0, 0], k1c[0, 0, 0, 0:BLOCK]),
                    cat(v1p[0, 0, 0], v1c[0, 0, 0, 0:BLOCK]), bias_first)
    merge(pl.ds(0, BLOCK), part, True)

    def body1(i, carry):
        for cc in range(per1):
            c = 1 + i * per1 + cc
            lo = pl.multiple_of((c - 1) * BLOCK, BLOCK)
            mid = pl.multiple_of(c * BLOCK, BLOCK)
            part = partials(q1[0, 0, 0, pl.ds(mid, BLOCK)], k1c[0, 0, 0, pl.ds(lo, 2 * BLOCK)],
                            v1c[0, 0, 0, pl.ds(lo, 2 * BLOCK)], bias_in)
            merge(pl.ds(mid, BLOCK), part, True)
        return carry

    assert (nb1 - 1) % per1 == 0
    lax.fori_loop(0, (nb1 - 1) // per1, body1, 0)


def _attn_fused(qkvs):
    b = qkvs[0].shape[0]
    s = qkvs[0].shape[3]
    assert s % SUPER == 0
    qi = np.arange(BLOCK)[:, None]
    kj = np.arange(2 * BLOCK)[None, :]
    dist = qi + BLOCK - kj
    ok = (dist >= 0) & (dist <= BLOCK)
    bias = np.stack([np.where(ok & (kj >= BLOCK), 0.0, -np.inf), np.where(ok, 0.0, -np.inf)]).astype(np.float32)
    in_specs = [_resident(bias.shape)]
    args = [jnp.asarray(bias)]
    for g in (2, 1, 0):
        dil = GROUPS[g][1]
        rows = SUPER // dil
        per_super = rows // BLOCK

        def cur(which, dil=dil, rows=rows):
            return pl.BlockSpec((1, 1, dil, rows, LANES), lambda bi, j, p: (bi, which * PAIRS + p, 0, j, 0))

        def prev(which, dil=dil, per_super=per_super):
            return pl.BlockSpec((1, 1, dil, BLOCK, LANES),
                                lambda bi, j, p: (bi, which * PAIRS + p, 0, jnp.maximum(j * per_super - 1, 0), 0))

        in_specs += [cur(0), cur(1), prev(1), cur(2), prev(2)]
        args += [qkvs[g]] * 5
    return pl.pallas_call(
        _attn_fused_kernel,
        grid=(b, s // SUPER, PAIRS),
        in_specs=in_specs,
        out_specs=pl.BlockSpec((1, SUPER, LANES), lambda bi, j, p: (bi, j, p)),
        out_shape=jax.ShapeDtypeStruct((b, s, ATT_W), BF16),
        scratch_shapes=[pltpu.VMEM((SUPER, LANES), F32)] * 3,
        compiler_params=_params(("parallel", "parallel", "parallel")),
        name="attn_fused",
    )(*args)


def _mixer_tail(y1, att, conv, gates, cb, lng, lnb, w_conv_out, w_att_out, w_o):
    a = jnp.dot(att, w_att_out, preferred_element_type=F32)
    c = conv + cb
    mu = jnp.mean(c, axis=-1, keepdims=True)
    xc = c - mu
    cn = xc * lax.rsqrt(jnp.mean(xc * xc, axis=-1, keepdims=True) + LN_EPS) * lng + lnb
    cs = (cn * jax.nn.sigmoid(cn)).astype(BF16)
    cbr = jnp.dot(cs, w_conv_out, preferred_element_type=F32)
    ga = gates[:, :D_MODEL].astype(F32)
    gc = gates[:, D_MODEL:].astype(F32)
    merged = (ga * a + gc * cbr).astype(BF16)
    return y1 + jnp.dot(merged, w_o, preferred_element_type=F32)


def _tail_kernel(y1_ref, *refs, combine):
    if combine:
        o0, o1, o2, l0, l1, l2 = refs[:6]
        refs = refs[6:]
        lses = (l0[...], l1[...], l2[...])
        m = jnp.maximum(jnp.maximum(lses[0], lses[1]), lses[2])
        es = [jnp.exp(l - m) for l in lses]
        inv = 1.0 / (es[0] + es[1] + es[2])
        att = ((es[0] * inv) * o0[...] + (es[1] * inv) * o1[...] + (es[2] * inv) * o2[...]).astype(BF16)
    else:
        att = refs[0][...]
        refs = refs[1:]
    conv_ref, gates_ref, cb_ref, lng_ref, lnb_ref, wco_ref, wao_ref, wo_ref, out_ref = refs
    out_ref[...] = _mixer_tail(y1_ref[...], att, conv_ref[...], gates_ref[...], cb_ref[...], lng_ref[...], lnb_ref[...],
                               wco_ref[...], wao_ref[...], wo_ref[...])


def _tail(y1, att_args, conv, gates, cb, lng, lnb, wco, wao, wo, *, tm):
    n = y1.shape[0]
    row = lambda w: pl.BlockSpec((tm, w), lambda i: (i, 0))
    return pl.pallas_call(
        functools.partial(_tail_kernel, combine=len(att_args) > 1),
        grid=(n // tm,),
        in_specs=[row(D_MODEL)] + [row(ATT_W)] * len(att_args) + [row(C_CONV), row(2 * D_MODEL), _resident(cb.shape),
                  _resident(lng.shape), _resident(lnb.shape), _resident(wco.shape), _resident(wao.shape),
                  _resident(wo.shape)],
        out_specs=row(D_MODEL),
        out_shape=jax.ShapeDtypeStruct((n, D_MODEL), F32),
        compiler_params=_params(("parallel",)),
        name="tail",
    )(y1, *att_args, conv, gates, cb, lng, lnb, wco, wao, wo)


def _proj_sample_kernel(x_ref, g_ref, w_ref, gb_ref, cos_ref, sin_ref, q_ref, u_ref, gates_ref):
    xn = _rms(x_ref[...], g_ref[...]).astype(BF16)
    cos = cos_ref[...]
    sin_signed = sin_ref[...]
    lane = lax.broadcasted_iota(jnp.int32, cos.shape, 1)
    first_half = jnp.bitwise_and(lane, HEAD_DIM // 2) == 0
    for g in range(N_GROUPS):
        for p in range(PAIRS):
            col = (g * CHUNKS + p) * LANES
            q = jnp.dot(xn, w_ref[:, col:col + LANES], preferred_element_type=F32)
            q_ref[:, (g * PAIRS + p) * LANES:(g * PAIRS + p + 1) * LANES] = _rope(q, cos, sin_signed, first_half) * SCALE
    for c in range(C_CONV // 256):
        a = jnp.dot(xn, w_ref[:, N_ATT_COLS + c * 256:N_ATT_COLS + (c + 1) * 256], preferred_element_type=F32)
        b = jnp.dot(xn, w_ref[:, N_ATT_COLS + C_CONV + c * 256:N_ATT_COLS + C_CONV + (c + 1) * 256],
                    preferred_element_type=F32)
        u_ref[:, c * 256:(c + 1) * 256] = a * jax.nn.sigmoid(b)
    base = N_ATT_COLS + 2 * C_CONV
    for c in range(2 * D_MODEL // 256):
        gl = jnp.dot(xn, w_ref[:, base + c * 256:base + (c + 1) * 256], preferred_element_type=F32)
        gates_ref[:, c * 256:(c + 1) * 256] = jax.nn.sigmoid(gl + gb_ref[:, c * 256:(c + 1) * 256]).astype(BF16)


def _proj_sample(x, norm_g, w_in, gate_bias, cos, sin_signed, *, tm):
    n = x.shape[0]
    row = lambda w: pl.BlockSpec((tm, w), lambda i: (i, 0))
    return pl.pallas_call(
        _proj_sample_kernel,
        grid=(n // tm,),
        in_specs=[row(D_MODEL), _resident((1, D_MODEL)), _resident(w_in.shape), _resident((1, 2 * D_MODEL)),
                  row(LANES), row(LANES)],
        out_specs=[row(N_GROUPS * ATT_W), row(C_CONV), row(2 * D_MODEL)],
        out_shape=[jax.ShapeDtypeStruct((n, N_GROUPS * ATT_W), F32), jax.ShapeDtypeStruct((n, C_CONV), F32),
                   jax.ShapeDtypeStruct((n, 2 * D_MODEL), BF16)],
        compiler_params=_params(("parallel",)),
        name="proj_sample",
    )(x, norm_g, w_in, gate_bias, cos, sin_signed)


def _proj_kvt_kernel(x_ref, g_ref, wt_ref, cos_ref, sin_ref, kvt_ref):
    xn = _rms(x_ref[...], g_ref[...]).astype(BF16)
    half = HEAD_DIM // 2
    for c in range(2 * PAIRS):
        r = lax.dot_general(wt_ref[0, c * LANES:(c + 1) * LANES, :], xn, NT, preferred_element_type=F32)
        if c < PAIRS:
            rot = jnp.concatenate([r[half:2 * half], r[0:half], r[3 * half:4 * half], r[2 * half:3 * half]], axis=0)
            r = r * cos_ref[...] + rot * sin_ref[...]
        kvt_ref[0, c * LANES:(c + 1) * LANES, :] = r


def _proj_kvt(x, norm_g, w_kv_t, cos_t, sin_t):
    n = x.shape[0]
    return pl.pallas_call(
        _proj_kvt_kernel,
        grid=(N_GROUPS,),
        in_specs=[_resident(x.shape), _resident((1, D_MODEL)),
                  pl.BlockSpec((1, 2 * ATT_W, D_MODEL), lambda g: (g, 0, 0)), _resident(cos_t.shape),
                  _resident(sin_t.shape)],
        out_specs=pl.BlockSpec((1, 2 * ATT_W, n), lambda g: (g, 0, 0)),
        out_shape=jax.ShapeDtypeStruct((N_GROUPS, 2 * ATT_W, n), F32),
        compiler_params=_params(("parallel",)),
        name="proj_kvt",
    )(x, norm_g, w_kv_t, cos_t, sin_t)


def _attn_sample_kernel(cache_ref, q_ref, new_ref, cache_out_ref, o_ref, lse_ref, *, dil, width, t_new, bb):
    tq = q_ref.shape[1]
    ext = width + LANES
    step = pl.program_id(0)
    qrow = lax.broadcasted_iota(jnp.int32, (2 * tq, ext), 0)
    kcol = lax.broadcasted_iota(jnp.int32, (2 * tq, ext), 1)
    t_q = jnp.bitwise_and(qrow, tq - 1)
    kpos = jnp.where(kcol < width, kcol, kcol - (LANES - t_new))
    dist = width + t_q - kpos
    ok = (jnp.bitwise_and(dist, dil - 1) == 0) & (dist <= BLOCK * dil) & (dist >= 0)
    ok = ok & ((kcol < width) | (kcol >= ext - t_new))
    lane = lax.broadcasted_iota(jnp.int32, (2 * tq, LANES), 1)
    row = lax.broadcasted_iota(jnp.int32, (2 * tq, LANES), 0)
    mine = (lane < HEAD_DIM) == (row < tq)
    lane_t = lax.broadcasted_iota(jnp.int32, (LANES, LANES), 1)
    is_new = lane_t >= LANES - t_new
    per_tile = LANES // t_new
    for i in range(bb):
        brow = step * bb + i
        shift = (LANES - t_new) - t_new * jnp.bitwise_and(brow, per_tile - 1)
        for p in range(PAIRS):
            sl = slice(p * LANES, (p + 1) * LANES)
            kt = cache_ref[i, 0, 2 * p:2 * p + 2].reshape(LANES, width)
            vt = cache_ref[i, 1, 2 * p:2 * p + 2].reshape(LANES, width)
            nk = jnp.where(is_new, pltpu.roll(new_ref[p * LANES:(p + 1) * LANES, :], shift, axis=1), 0.0)
            nv = jnp.where(is_new, pltpu.roll(new_ref[ATT_W + p * LANES:ATT_W + (p + 1) * LANES, :], shift, axis=1),
                           0.0)
            for kv, old, new in ((0, kt, nk), (1, vt, nv)):
                rolled = pltpu.roll(old, width - t_new, axis=1)
                last = jnp.where(is_new, new, rolled[:, width - LANES:])
                if width > LANES:
                    last = jnp.concatenate([rolled[:, :width - LANES], last], axis=1)
                cache_out_ref[i, kv, 2 * p:2 * p + 2] = last.reshape(2, HEAD_DIM, width)
            kext = jnp.concatenate([kt, nk], axis=1).astype(BF16)
            vext = jnp.concatenate([vt, nv], axis=1).astype(BF16)
            q = q_ref[i, :, sl]
            q2 = jnp.where(mine, jnp.concatenate([q, q], axis=0), 0.0).astype(BF16)
            s = jnp.where(ok, jnp.dot(q2, kext, preferred_element_type=F32), -jnp.inf)
            m = jnp.max(s, axis=-1, keepdims=True)
            pr = jnp.exp(s - m)
            l = jnp.sum(pr, axis=-1, keepdims=True)
            acc = lax.dot_general(pr.astype(BF16), vext, NT, preferred_element_type=F32) * (1.0 / l)
            lse = m + jnp.log(l)
            low = lane[:tq] < HEAD_DIM
            o_ref[i, :, sl] = jnp.where(low, acc[:tq], acc[tq:])
            lse_ref[i, :, sl] = jnp.where(low, lse[:tq], lse[tq:])


def _attn_sample(cache_t, q_pad, new_t, g, dil, t_new, bb):
    db, _, _, _, width = cache_t.shape
    tq = q_pad.shape[1]
    assert dil & (dil - 1) == 0 and tq & (tq - 1) == 0 and LANES % t_new == 0 and (LANES // t_new) % bb == 0
    per_tile = LANES // t_new
    cblk = pl.BlockSpec((bb, 2, H_G, HEAD_DIM, width), lambda i: (i, 0, 0, 0, 0))
    oblk = pl.BlockSpec((bb, tq, ATT_W), lambda i: (i, 0, 0))
    return pl.pallas_call(
        functools.partial(_attn_sample_kernel, dil=dil, width=width, t_new=t_new, bb=bb),
        grid=(db // bb,),
        in_specs=[cblk, pl.BlockSpec((bb, tq, ATT_W), lambda i: (i, 0, g)),
                  pl.BlockSpec((None, 2 * ATT_W, LANES), lambda i: (g, 0, (i * bb) // per_tile))],
        out_specs=[cblk, oblk, oblk],
        out_shape=[jax.ShapeDtypeStruct(cache_t.shape, F32), jax.ShapeDtypeStruct((db, tq, ATT_W), F32),
                   jax.ShapeDtypeStruct((db, tq, ATT_W), F32)],
        compiler_params=_params(("parallel",)),
        name="attn_sample_d%d" % dil,
    )(cache_t, q_pad, new_t)


CONV_LANES = 256


def _conv_sample_kernel(u_ref, st_ref, cw_ref, conv_ref, st_out_ref, *, db, t_new):
    hist = CONV_WIDTH - 1

    def slab(i):
        return st_ref[i] if i < hist else u_ref[(i - hist) * db:(i - hist + 1) * db, :]

    for t in range(t_new):
        acc = jnp.zeros((db, CONV_LANES), F32)
        for j in range(CONV_WIDTH):
            acc = acc + cw_ref[j:j + 1, :] * slab(t + j)
        conv_ref[t * db:(t + 1) * db, :] = acc
    for i in range(hist):
        st_out_ref[i] = slab(i + t_new)


def _conv_sample(u, state_t, cw, *, db, t_new):
    hist = CONV_WIDTH - 1
    st = pl.BlockSpec((hist, db, CONV_LANES), lambda c: (0, 0, c))
    col = lambda rows: pl.BlockSpec((rows, CONV_LANES), lambda c: (0, c))
    return pl.pallas_call(
        functools.partial(_conv_sample_kernel, db=db, t_new=t_new),
        grid=(C_CONV // CONV_LANES,),
        in_specs=[col(t_new * db), st, col(cw.shape[0])],
        out_specs=[col(t_new * db), st],
        out_shape=[jax.ShapeDtypeStruct(u.shape, F32), jax.ShapeDtypeStruct(state_t.shape, F32)],
        compiler_params=_params(("parallel",)),
        name="conv_sample",
    )(u, state_t, cw)


def kernel(x_prompt, x_sample, cache_kv_w128, cache_kv_w512, cache_kv_w2048, state_conv, ffn1_norm, ffn1_w_gate,
           ffn1_w_up, ffn1_w_down, mix_norm, w_in, gate_bias, conv_w, conv_b, conv_ln_g, conv_ln_b, w_conv_out,
           w_att_out, w_o, ffn2_norm, ffn2_w_gate, ffn2_w_up, ffn2_w_down, final_norm):
    assert ffn1_norm.shape[0] == 1, "single layer"
    b, s, _ = x_prompt.shape
    db, t_new, _ = x_sample.shape
    caches = (cache_kv_w128, cache_kv_w512, cache_kv_w2048)
    bf = lambda w: w[0].astype(BF16)
    ffn1 = (ffn1_norm, bf(ffn1_w_gate), bf(ffn1_w_up), bf(ffn1_w_down))
    ffn2 = (ffn2_norm, bf(ffn2_w_gate), bf(ffn2_w_up), bf(ffn2_w_down))
    w_in_b, wco, wao, wo = bf(w_in), bf(w_conv_out), bf(w_att_out), bf(w_o)
    w_att = w_in_b[:, :N_ATT_COLS]
    w_cg = w_in_b[:, N_ATT_COLS:]
    kv_cols = np.concatenate([np.arange((3 * g + 1) * ATT_W, (3 * g + 3) * ATT_W) for g in range(N_GROUPS)])
    w_kv_t = w_in_b[:, kv_cols].T.reshape(N_GROUPS, 2 * ATT_W, D_MODEL)
    cw = jnp.pad(conv_w[0], ((0, HALO - CONV_WIDTH), (0, 0)))
    fin = final_norm[None, :]
    tail_w = (conv_b, conv_ln_g, conv_ln_b, wco, wao, wo)

    tm = 512
    y1p = _ffn(x_prompt.reshape(b * s, D_MODEL), *ffn1, fin, final=False, tm=tm)
    y1p3 = y1p.reshape(b, s, D_MODEL)
    cos_p, sin_p = _rope_tables(jnp.arange(s, dtype=jnp.int32))
    q0, q1, q2, kt0, kt1, kt2 = _proj_att(y1p3, mix_norm, w_att, cos_p, sin_p, tm=tm)
    conv_p, gates_p, utail = _proj_conv(y1p3, mix_norm, w_cg, gate_bias, cw, tm=tm)
    att_p = _attn_fused((q0, q1, q2))
    y2p = _tail(y1p, (att_p.reshape(b * s, ATT_W),), conv_p.reshape(b * s, C_CONV),
                gates_p.reshape(b * s, 2 * D_MODEL), *tail_w, tm=tm)
    y_prompt = _ffn(y2p, *ffn2, fin, final=True, tm=tm).reshape(b, s, D_MODEL)
    kv_prompt = [kt.reshape(1, b, kt.shape[1], 2, H_G, HEAD_DIM) for kt in (kt0, kt1, kt2)]
    conv_prompt = utail[None, :, HALO - (CONV_WIDTH - 1):, :]

    n_s = db * t_new
    y1s = _ffn(x_sample.reshape(n_s, D_MODEL), *ffn1, fin, final=False, tm=n_s)
    pos_s = PAST_LEN + jnp.tile(jnp.arange(t_new, dtype=jnp.int32), db)
    cos_s, sin_s = _rope_tables(pos_s)
    q_s, u_s, gates_s = _proj_sample(y1s, mix_norm, w_in_b, gate_bias, cos_s, sin_s, tm=n_s // 2)
    new_t = _proj_kvt(y1s, mix_norm, w_kv_t, cos_s.T, sin_s.T)
    q_pad = jnp.pad(q_s.reshape(db, t_new, N_GROUPS * ATT_W), ((0, 0), (0, SAMPLE_Q_ROWS - t_new), (0, 0)))
    att_args, lses, kv_sample = [], [], []
    for g, (win, dil) in enumerate(GROUPS):
        cache_t = jnp.transpose(caches[g][0], (0, 2, 3, 4, 1))
        bb = max(1, min(8, 1024 // cache_t.shape[-1]))
        shifted, o, l = _attn_sample(cache_t, q_pad, new_t, g, dil, t_new, bb)
        kv_sample.append(jnp.transpose(shifted, (0, 4, 1, 2, 3))[None])
        att_args.append(o[:, :t_new].reshape(n_s, ATT_W))
        lses.append(l[:, :t_new].reshape(n_s, ATT_W))
    to_token_major = lambda a: jnp.swapaxes(a.reshape(db, t_new, -1), 0, 1).reshape(n_s, -1)
    state_t = jnp.swapaxes(state_conv[0], 0, 1)
    conv_tm, state_new = _conv_sample(to_token_major(u_s), state_t, cw, db=db, t_new=t_new)
    conv_s = jnp.swapaxes(conv_tm.reshape(t_new, db, C_CONV), 0, 1).reshape(n_s, C_CONV)
    y2s = _tail(y1s, (*att_args, *lses), conv_s, gates_s, *tail_w, tm=n_s // 2)
    y_sample = _ffn(y2s, *ffn2, fin, final=True, tm=n_s).reshape(db, t_new, D_MODEL)
    conv_sample = jnp.swapaxes(state_new, 0, 1)[None]

    return (y_prompt, y_sample, kv_prompt[0], kv_prompt[1], kv_prompt[2], conv_prompt,
            kv_sample[0], kv_sample[1], kv_sample[2], conv_sample)
```

```python
import functools

import numpy as np
import jax
import jax.numpy as jnp
from jax import lax
from jax.experimental import pallas as pl
from jax.experimental.pallas import tpu as pltpu

D_MODEL = 1024
D_FF = 2816
GROUPS = ((128, 1), (512, 4), (2048, 16))
N_GROUPS = len(GROUPS)
H_G = 8
HEAD_DIM = 64
ATT_W = H_G * HEAD_DIM
BLOCK = 128
C_CONV = D_MODEL
CONV_WIDTH = 31
N_ATT_COLS = N_GROUPS * 3 * ATT_W
N_IN = N_ATT_COLS + 2 * C_CONV + 2 * D_MODEL
ROPE_THETA = 10000.0
PAST_LEN = 8192
RMS_EPS = 1e-6
LN_EPS = 1e-5
SCALE = HEAD_DIM ** -0.5

LANES = 128
SUBLANES = 8
HALO = 32
SAMPLE_Q_ROWS = 8
PAIRS = ATT_W // LANES
CHUNKS = 3 * PAIRS
SUPER = BLOCK * max(d for _, d in GROUPS)
VMEM_LIMIT = 56 * 1024 * 1024

F32 = jnp.float32
BF16 = jnp.bfloat16
NT = (((1,), (1,)), ((), ()))


def _params(sem):
    return pltpu.CompilerParams(dimension_semantics=sem, vmem_limit_bytes=VMEM_LIMIT)


def _resident(shape):
    nd = len(shape)
    return pl.BlockSpec(shape, lambda *_: (0,) * nd, pipeline_mode=pl.Buffered(1))


def _rms(x, g):
    return x * lax.rsqrt(jnp.mean(x * x, axis=-1, keepdims=True) + RMS_EPS) * g


FF_CHUNK = 256


def _ffn_kernel(x_ref, g_ref, wg_ref, wu_ref, wd_ref, fin_ref, o_ref, acc_ref, *, final):
    x = x_ref[...]
    h = _rms(x, g_ref[...]).astype(BF16)
    for j in range(D_FF // FF_CHUNK):
        sl = slice(j * FF_CHUNK, (j + 1) * FF_CHUNK)
        gate = jnp.dot(h, wg_ref[:, sl], preferred_element_type=F32)
        up = jnp.dot(h, wu_ref[:, sl], preferred_element_type=F32)
        act = (gate * jax.nn.sigmoid(gate) * up).astype(BF16)
        part = jnp.dot(act, wd_ref[sl, :], preferred_element_type=F32)
        if j == 0:
            acc_ref[...] = part
        else:
            acc_ref[...] += part
    y = x + 0.5 * acc_ref[...]
    if final:
        y = _rms(y, fin_ref[...])
    o_ref[...] = y


def _ffn(x, norm_g, wg, wu, wd, final_g, *, final, tm):
    n = x.shape[0]
    row = pl.BlockSpec((tm, D_MODEL), lambda i: (i, 0))
    return pl.pallas_call(
        functools.partial(_ffn_kernel, final=final),
        grid=(n // tm,),
        in_specs=[row, _resident((1, D_MODEL)), _resident(wg.shape), _resident(wu.shape), _resident(wd.shape),
                  _resident((1, D_MODEL))],
        out_specs=row,
        out_shape=jax.ShapeDtypeStruct((n, D_MODEL), F32),
        scratch_shapes=[pltpu.VMEM((tm, D_MODEL), F32)],
        compiler_params=_params(("parallel",)),
        name="ffn",
    )(x, norm_g, wg, wu, wd, final_g)


def _rope(x, cos, sin_signed, first_half):
    fwd = pltpu.roll(x, LANES - HEAD_DIM // 2, axis=1)
    bwd = pltpu.roll(x, HEAD_DIM // 2, axis=1)
    return x * cos + jnp.where(first_half, fwd, bwd) * sin_signed


def _rope_tables(positions):
    half = HEAD_DIM // 2
    inv_freq = ROPE_THETA ** (-jnp.arange(half, dtype=F32) * 2.0 / HEAD_DIM)
    ang = positions.astype(F32)[:, None] * inv_freq[None, :]
    cos = jnp.tile(jnp.cos(ang), (1, LANES // half))
    sin = jnp.tile(jnp.concatenate([-jnp.sin(ang), jnp.sin(ang)], axis=1), (1, LANES // HEAD_DIM))
    return cos, sin


def _proj_att_kernel(x_ref, g_ref, w_ref, cos_ref, sin_ref, q0_ref, q1_ref, q2_ref, t0_ref, t1_ref, t2_ref,
                     stage_ref, *, tm):
    xn = _rms(x_ref[0], g_ref[...]).astype(BF16)
    cos = cos_ref[...]
    sin_signed = sin_ref[...]
    lane = lax.broadcasted_iota(jnp.int32, cos.shape, 1)
    first_half = jnp.bitwise_and(lane, HEAD_DIM // 2) == 0
    outs = (q0_ref, q1_ref, q2_ref)
    tails = (t0_ref, t1_ref, t2_ref)
    for c in range(N_GROUPS * CHUNKS):
        g, lc = divmod(c, CHUNKS)
        which = lc // PAIRS
        dil = GROUPS[g][1]
        if c % 2 == 0:
            wide = jnp.dot(xn, w_ref[:, c * LANES:(c + 2) * LANES], preferred_element_type=F32)
        p = wide[:, (c % 2) * LANES:(c % 2 + 1) * LANES]
        if which < 2:
            p = _rope(p, cos, sin_signed, first_half)
        if which == 0:
            p = p * SCALE
        else:
            rows = tails[g].shape[1]
            tails[g][0, :, (lc - PAIRS) * LANES:(lc - PAIRS + 1) * LANES] = p[tm - rows:, :]
        if dil == 1:
            outs[g][0, lc, 0] = p.astype(BF16)
        else:
            stage_ref[...] = p
            for r in range(dil):
                outs[g][0, lc, r] = stage_ref[pl.ds(r, tm // dil, stride=dil), :].astype(BF16)


def _proj_att(y1, norm_g, w_att, cos, sin_signed, *, tm):
    b, s, _ = y1.shape
    tiles = s // tm
    tab = pl.BlockSpec((tm, LANES), lambda bi, t: (t, 0))
    in_specs = [pl.BlockSpec((1, tm, D_MODEL), lambda bi, t: (bi, t, 0)), _resident((1, D_MODEL)),
                _resident(w_att.shape), tab, tab]
    out_specs, out_shape = [], []
    for _, dil in GROUPS:
        out_specs.append(pl.BlockSpec((1, CHUNKS, dil, tm // dil, LANES), lambda bi, t: (bi, 0, 0, t, 0)))
        out_shape.append(jax.ShapeDtypeStruct((b, CHUNKS, dil, s // dil, LANES), BF16))
    for win, _ in GROUPS:
        keep = min(win, s)
        rows = min(tm, keep)
        first_tile = tiles - keep // rows
        out_specs.append(pl.BlockSpec((1, rows, 2 * ATT_W),
                                      lambda bi, t, ft=first_tile: (bi, jnp.maximum(t - ft, 0), 0)))
        out_shape.append(jax.ShapeDtypeStruct((b, keep, 2 * ATT_W), F32))
    return pl.pallas_call(
        functools.partial(_proj_att_kernel, tm=tm),
        grid=(b, tiles),
        in_specs=in_specs,
        out_specs=out_specs,
        out_shape=out_shape,
        scratch_shapes=[pltpu.VMEM((tm, LANES), F32)],
        compiler_params=_params(("parallel", "arbitrary")),
        name="proj_att",
    )(y1, norm_g, w_att, cos, sin_signed)


CONV_ROWS = 64


def _proj_conv_kernel(x_ref, g_ref, w_ref, gb_ref, cw_ref, c_ref, gates_ref, utail_ref, uf_ref, sh_ref, *, tm):
    t = pl.program_id(1)
    xn = _rms(x_ref[0], g_ref[...]).astype(BF16)

    @pl.when(t == 0)
    def _():
        uf_ref[0:HALO, :] = jnp.zeros((HALO, C_CONV), F32)

    @pl.when(t > 0)
    def _():
        uf_ref[0:HALO, :] = uf_ref[tm:tm + HALO, :]

    def glu(c):
        a = jnp.dot(xn, w_ref[:, c * 256:(c + 1) * 256], preferred_element_type=F32)
        b = jnp.dot(xn, w_ref[:, C_CONV + c * 256:C_CONV + (c + 1) * 256], preferred_element_type=F32)
        uf_ref[HALO:HALO + tm, c * 256:(c + 1) * 256] = a * jax.nn.sigmoid(b)

    off = HALO - (CONV_WIDTH - 1)
    sh_rows = sh_ref.shape[1]

    def conv(lc):
        ls = slice(lc * LANES, (lc + 1) * LANES)
        for ph in range(1, SUBLANES):
            sh_ref[ph - 1] = uf_ref[pl.ds(ph, sh_rows), ls]
        for rc in range(tm // CONV_ROWS):
            acc = jnp.zeros((CONV_ROWS, LANES), F32)
            for j in range(CONV_WIDTH):
                a8, ph = divmod(off + j, SUBLANES)
                r0 = rc * CONV_ROWS + a8 * SUBLANES
                slab = uf_ref[r0:r0 + CONV_ROWS, ls] if ph == 0 else sh_ref[ph - 1, r0:r0 + CONV_ROWS, :]
                acc = acc + cw_ref[j:j + 1, ls] * slab
            c_ref[0, rc * CONV_ROWS:(rc + 1) * CONV_ROWS, ls] = acc

    base = 2 * C_CONV

    def gate(c):
        gl = jnp.dot(xn, w_ref[:, base + c * 256:base + (c + 1) * 256], preferred_element_type=F32)
        gates_ref[0, :, c * 256:(c + 1) * 256] = jax.nn.sigmoid(gl + gb_ref[:, c * 256:(c + 1) * 256]).astype(BF16)

    n_glu = C_CONV // 256
    glu(0)
    for c in range(n_glu):
        if c + 1 < n_glu:
            glu(c + 1)
        else:
            utail_ref[0] = uf_ref[tm:tm + HALO, :]
        for k in range(2):
            conv(2 * c + k)
            gate(2 * c + k)


def _proj_conv(y1, norm_g, w_cg, gate_bias, cw, *, tm):
    b, s, _ = y1.shape
    row = lambda w: pl.BlockSpec((1, tm, w), lambda bi, t: (bi, t, 0))
    sh_rows = tm + HALO - SUBLANES
    return pl.pallas_call(
        functools.partial(_proj_conv_kernel, tm=tm),
        grid=(b, s // tm),
        in_specs=[row(D_MODEL), _resident((1, D_MODEL)), _resident(w_cg.shape), _resident((1, 2 * D_MODEL)),
                  _resident(cw.shape)],
        out_specs=[row(C_CONV), row(2 * D_MODEL), pl.BlockSpec((1, HALO, C_CONV), lambda bi, t: (bi, 0, 0))],
        out_shape=[jax.ShapeDtypeStruct((b, s, C_CONV), F32), jax.ShapeDtypeStruct((b, s, 2 * D_MODEL), BF16),
                   jax.ShapeDtypeStruct((b, HALO, C_CONV), F32)],
        scratch_shapes=[pltpu.VMEM((tm + HALO, C_CONV), F32), pltpu.VMEM((SUBLANES - 1, sh_rows, LANES), F32)],
        compiler_params=_params(("parallel", "arbitrary")),
        name="proj_conv",
    )(y1, norm_g, w_cg, gate_bias, cw)


def _attn_fused_kernel(bias_ref, q3, k3c, k3p, v3c, v3p, q2, k2c, k2p, v2c, v2p, q1, k1c, k1p, v1c, v1p,
                       o_ref, acc_ref, m_ref, l_ref):
    first = pl.program_id(1) == 0
    bias_in = bias_ref[1]
    bias_first = jnp.where(first, bias_ref[0], bias_in)
    lane = lax.broadcasted_iota(jnp.int32, (BLOCK, LANES), 1)
    low = lane < HEAD_DIM

    def partials_many(items):
        keeps = (low, jnp.logical_not(low))
        scores = [[lax.dot_general(jnp.where(keep, q, jnp.zeros_like(q)), kk, NT, preferred_element_type=F32) + bias
                   for keep in keeps] for q, kk, _, bias in items]
        stats = []
        for pair in scores:
            row = []
            for s in pair:
                m = jnp.max(s, axis=-1, keepdims=True)
                p = jnp.exp(s - m)
                row.append((m, jnp.sum(p, axis=-1, keepdims=True), p.astype(BF16)))
            stats.append(row)
        parts = []
        for (_, _, vv, _), row in zip(items, stats):
            res = [(m, l, jnp.dot(p, vv, preferred_element_type=F32)) for m, l, p in row]
            parts.append(tuple(jnp.where(low, a, b) for a, b in zip(res[0], res[1])))
        return parts

    def partials(q, kk, vv, bias):
        return partials_many([(q, kk, vv, bias)])[0]

    def init(rows, part):
        m_g, l_g, pv_g = part
        m_ref[rows, :] = m_g
        l_ref[rows, :] = l_g
        acc_ref[rows, :] = pv_g

    def merge(rows, part, final):
        m_g, l_g, pv_g = part
        m_o = m_ref[rows, :]
        m_n = jnp.maximum(m_o, m_g)
        a = jnp.exp(m_o - m_n)
        b = jnp.exp(m_g - m_n)
        l_n = a * l_ref[rows, :] + b * l_g
        acc_n = a * acc_ref[rows, :] + b * pv_g
        if final:
            o_ref[0, rows, :] = (acc_n * (1.0 / l_n)).astype(BF16)
        else:
            m_ref[rows, :] = m_n
            l_ref[rows, :] = l_n
            acc_ref[rows, :] = acc_n

    def cat(prev, cur):
        return jnp.concatenate([prev, cur], axis=0)

    d3 = GROUPS[2][1]
    per3 = 8

    def body3(i, carry):
        rs = [i * per3 + rr for rr in range(per3)]
        parts = partials_many([(q3[0, 0, r], cat(k3p[0, 0, r], k3c[0, 0, r]), cat(v3p[0, 0, r], v3c[0, 0, r]),
                                bias_first) for r in rs])
        for r, part in zip(rs, parts):
            init(pl.ds(r, BLOCK, stride=d3), part)
        return carry

    lax.fori_loop(0, d3 // per3, body3, 0)

    d2 = GROUPS[1][1]
    nb2 = SUPER // d2 // BLOCK
    parts = partials_many([(q2[0, 0, r, 0:BLOCK], cat(k2p[0, 0, r], k2c[0, 0, r, 0:BLOCK]),
                            cat(v2p[0, 0, r], v2c[0, 0, r, 0:BLOCK]), bias_first) for r in range(d2)])
    for r, part in enumerate(parts):
        merge(pl.ds(r, BLOCK, stride=d2), part, False)

    def body2(c, carry):
        lo = pl.multiple_of((c - 1) * BLOCK, BLOCK)
        mid = pl.multiple_of(c * BLOCK, BLOCK)
        parts = partials_many([(q2[0, 0, r, pl.ds(mid, BLOCK)], k2c[0, 0, r, pl.ds(lo, 2 * BLOCK)],
                                v2c[0, 0, r, pl.ds(lo, 2 * BLOCK)], bias_in) for r in range(d2)])
        for r, part in enumerate(parts):
            merge(pl.ds(c * (BLOCK * d2) + r, BLOCK, stride=d2), part, False)
        return carry

    lax.fori_loop(1, nb2, body2, 0)

    nb1 = SUPER // BLOCK
    per1 = 5
    part = partials(q1[0, 0, 0, 0:BLOCK], cat(k1p[0, 0, 0], k1c[0, 0, 0, 0:BLOCK]),
                    cat(v1p[0, 0, 0], v1c[0, 0, 0, 0:BLOCK]), bias_first)
    merge(pl.ds(0, BLOCK), part, True)

    def body1(i, carry):
        mids = [pl.multiple_of((1 + i * per1 + cc) * BLOCK, BLOCK) for cc in range(per1)]
        los = [pl.multiple_of((i * per1 + cc) * BLOCK, BLOCK) for cc in range(per1)]
        parts = partials_many([(q1[0, 0, 0, pl.ds(mid, BLOCK)], k1c[0, 0, 0, pl.ds(lo, 2 * BLOCK)],
                                v1c[0, 0, 0, pl.ds(lo, 2 * BLOCK)], bias_in) for mid, lo in zip(mids, los)])
        for mid, part in zip(mids, parts):
            merge(pl.ds(mid, BLOCK), part, True)
        return carry

    assert (nb1 - 1) % per1 == 0
    lax.fori_loop(0, (nb1 - 1) // per1, body1, 0)


def _attn_fused(qkvs):
    b = qkvs[0].shape[0]
    s = qkvs[0].shape[3]
    assert s % SUPER == 0
    qi = np.arange(BLOCK)[:, None]
    kj = np.arange(2 * BLOCK)[None, :]
    dist = qi + BLOCK - kj
    ok = (dist >= 0) & (dist <= BLOCK)
    bias = np.stack([np.where(ok & (kj >= BLOCK), 0.0, -np.inf), np.where(ok, 0.0, -np.inf)]).astype(np.float32)
    in_specs = [_resident(bias.shape)]
    args = [jnp.asarray(bias)]
    for g in (2, 1, 0):
        dil = GROUPS[g][1]
        rows = SUPER // dil
        per_super = rows // BLOCK

        def cur(which, dil=dil, rows=rows):
            return pl.BlockSpec((1, 1, dil, rows, LANES), lambda bi, j, p: (bi, which * PAIRS + p, 0, j, 0))

        def prev(which, dil=dil, per_super=per_super):
            return pl.BlockSpec((1, 1, dil, BLOCK, LANES),
                                lambda bi, j, p: (bi, which * PAIRS + p, 0, jnp.maximum(j * per_super - 1, 0), 0))

        in_specs += [cur(0), cur(1), prev(1), cur(2), prev(2)]
        args += [qkvs[g]] * 5
    return pl.pallas_call(
        _attn_fused_kernel,
        grid=(b, s // SUPER, PAIRS),
        in_specs=in_specs,
        out_specs=pl.BlockSpec((1, SUPER, LANES), lambda bi, j, p: (bi, j, p)),
        out_shape=jax.ShapeDtypeStruct((b, s, ATT_W), BF16),
        scratch_shapes=[pltpu.VMEM((SUPER, LANES), F32)] * 3,
        compiler_params=_params(("parallel", "parallel", "parallel")),
        name="attn_fused",
    )(*args)


def _mixer_tail(y1, att, conv, gates, cb, lng, lnb, w_conv_out, w_att_out, w_o):
    a = jnp.dot(att, w_att_out, preferred_element_type=F32)
    c = conv + cb
    mu = jnp.mean(c, axis=-1, keepdims=True)
    xc = c - mu
    cn = xc * lax.rsqrt(jnp.mean(xc * xc, axis=-1, keepdims=True) + LN_EPS) * lng + lnb
    cs = (cn * jax.nn.sigmoid(cn)).astype(BF16)
    cbr = jnp.dot(cs, w_conv_out, preferred_element_type=F32)
    ga = gates[:, :D_MODEL].astype(F32)
    gc = gates[:, D_MODEL:].astype(F32)
    merged = (ga * a + gc * cbr).astype(BF16)
    return y1 + jnp.dot(merged, w_o, preferred_element_type=F32)


def _tail_kernel(y1_ref, *refs, combine):
    if combine:
        o0, o1, o2, l0, l1, l2 = refs[:6]
        refs = refs[6:]
        lses = (l0[...], l1[...], l2[...])
        m = jnp.maximum(jnp.maximum(lses[0], lses[1]), lses[2])
        es = [jnp.exp(l - m) for l in lses]
        inv = 1.0 / (es[0] + es[1] + es[2])
        att = ((es[0] * inv) * o0[...] + (es[1] * inv) * o1[...] + (es[2] * inv) * o2[...]).astype(BF16)
    else:
        att = refs[0][...]
        refs = refs[1:]
    conv_ref, gates_ref, cb_ref, lng_ref, lnb_ref, wco_ref, wao_ref, wo_ref, out_ref = refs
    out_ref[...] = _mixer_tail(y1_ref[...], att, conv_ref[...], gates_ref[...], cb_ref[...], lng_ref[...], lnb_ref[...],
                               wco_ref[...], wao_ref[...], wo_ref[...])


def _tail(y1, att_args, conv, gates, cb, lng, lnb, wco, wao, wo, *, tm):
    n = y1.shape[0]
    row = lambda w: pl.BlockSpec((tm, w), lambda i: (i, 0))
    return pl.pallas_call(
        functools.partial(_tail_kernel, combine=len(att_args) > 1),
        grid=(n // tm,),
        in_specs=[row(D_MODEL)] + [row(ATT_W)] * len(att_args) + [row(C_CONV), row(2 * D_MODEL), _resident(cb.shape),
                  _resident(lng.shape), _resident(lnb.shape), _resident(wco.shape), _resident(wao.shape),
                  _resident(wo.shape)],
        out_specs=row(D_MODEL),
        out_shape=jax.ShapeDtypeStruct((n, D_MODEL), F32),
        compiler_params=_params(("parallel",)),
        name="tail",
    )(y1, *att_args, conv, gates, cb, lng, lnb, wco, wao, wo)


def _proj_sample_kernel(x_ref, g_ref, w_ref, gb_ref, cos_ref, sin_ref, q_ref, u_ref, gates_ref):
    xn = _rms(x_ref[...], g_ref[...]).astype(BF16)
    cos = cos_ref[...]
    sin_signed = sin_ref[...]
    lane = lax.broadcasted_iota(jnp.int32, cos.shape, 1)
    first_half = jnp.bitwise_and(lane, HEAD_DIM // 2) == 0
    for g in range(N_GROUPS):
        for p in range(PAIRS):
            col = (g * CHUNKS + p) * LANES
            q = jnp.dot(xn, w_ref[:, col:col + LANES], preferred_element_type=F32)
            q_ref[:, (g * PAIRS + p) * LANES:(g * PAIRS + p + 1) * LANES] = _rope(q, cos, sin_signed, first_half) * SCALE
    for c in range(C_CONV // 256):
        a = jnp.dot(xn, w_ref[:, N_ATT_COLS + c * 256:N_ATT_COLS + (c + 1) * 256], preferred_element_type=F32)
        b = jnp.dot(xn, w_ref[:, N_ATT_COLS + C_CONV + c * 256:N_ATT_COLS + C_CONV + (c + 1) * 256],
                    preferred_element_type=F32)
        u_ref[:, c * 256:(c + 1) * 256] = a * jax.nn.sigmoid(b)
    base = N_ATT_COLS + 2 * C_CONV
    for c in range(2 * D_MODEL // 256):
        gl = jnp.dot(xn, w_ref[:, base + c * 256:base + (c + 1) * 256], preferred_element_type=F32)
        gates_ref[:, c * 256:(c + 1) * 256] = jax.nn.sigmoid(gl + gb_ref[:, c * 256:(c + 1) * 256]).astype(BF16)


def _proj_sample(x, norm_g, w_in, gate_bias, cos, sin_signed, *, tm):
    n = x.shape[0]
    row = lambda w: pl.BlockSpec((tm, w), lambda i: (i, 0))
    return pl.pallas_call(
        _proj_sample_kernel,
        grid=(n // tm,),
        in_specs=[row(D_MODEL), _resident((1, D_MODEL)), _resident(w_in.shape), _resident((1, 2 * D_MODEL)),
                  row(LANES), row(LANES)],
        out_specs=[row(N_GROUPS * ATT_W), row(C_CONV), row(2 * D_MODEL)],
        out_shape=[jax.ShapeDtypeStruct((n, N_GROUPS * ATT_W), F32), jax.ShapeDtypeStruct((n, C_CONV), F32),
                   jax.ShapeDtypeStruct((n, 2 * D_MODEL), BF16)],
        compiler_params=_params(("parallel",)),
        name="proj_sample",
    )(x, norm_g, w_in, gate_bias, cos, sin_signed)


def _proj_kvt_kernel(x_ref, g_ref, wt_ref, cos_ref, sin_ref, kvt_ref):
    xn = _rms(x_ref[...], g_ref[...]).astype(BF16)
    half = HEAD_DIM // 2
    for c in range(2 * PAIRS):
        r = lax.dot_general(wt_ref[0, c * LANES:(c + 1) * LANES, :], xn, NT, preferred_element_type=F32)
        if c < PAIRS:
            rot = jnp.concatenate([r[half:2 * half], r[0:half], r[3 * half:4 * half], r[2 * half:3 * half]], axis=0)
            r = r * cos_ref[...] + rot * sin_ref[...]
        kvt_ref[0, c * LANES:(c + 1) * LANES, :] = r


def _proj_kvt(x, norm_g, w_kv_t, cos_t, sin_t):
    n = x.shape[0]
    return pl.pallas_call(
        _proj_kvt_kernel,
        grid=(N_GROUPS,),
        in_specs=[_resident(x.shape), _resident((1, D_MODEL)),
                  pl.BlockSpec((1, 2 * ATT_W, D_MODEL), lambda g: (g, 0, 0)), _resident(cos_t.shape),
                  _resident(sin_t.shape)],
        out_specs=pl.BlockSpec((1, 2 * ATT_W, n), lambda g: (g, 0, 0)),
        out_shape=jax.ShapeDtypeStruct((N_GROUPS, 2 * ATT_W, n), F32),
        compiler_params=_params(("parallel",)),
        name="proj_kvt",
    )(x, norm_g, w_kv_t, cos_t, sin_t)


def _attn_sample_kernel(cache_ref, q_ref, new_ref, cache_out_ref, o_ref, lse_ref, *, dil, width, t_new, bb):
    tq = q_ref.shape[1]
    ext = width + LANES
    step = pl.program_id(0)
    qrow = lax.broadcasted_iota(jnp.int32, (2 * tq, ext), 0)
    kcol = lax.broadcasted_iota(jnp.int32, (2 * tq, ext), 1)
    t_q = jnp.bitwise_and(qrow, tq - 1)
    kpos = jnp.where(kcol < width, kcol, kcol - (LANES - t_new))
    dist = width + t_q - kpos
    ok = (jnp.bitwise_and(dist, dil - 1) == 0) & (dist <= BLOCK * dil) & (dist >= 0)
    ok = ok & ((kcol < width) | (kcol >= ext - t_new))
    lane = lax.broadcasted_iota(jnp.int32, (2 * tq, LANES), 1)
    row = lax.broadcasted_iota(jnp.int32, (2 * tq, LANES), 0)
    mine = (lane < HEAD_DIM) == (row < tq)
    lane_t = lax.broadcasted_iota(jnp.int32, (LANES, LANES), 1)
    is_new = lane_t >= LANES - t_new
    per_tile = LANES // t_new
    for i in range(bb):
        brow = step * bb + i
        shift = (LANES - t_new) - t_new * jnp.bitwise_and(brow, per_tile - 1)
        for p in range(PAIRS):
            sl = slice(p * LANES, (p + 1) * LANES)
            kt = cache_ref[i, 0, 2 * p:2 * p + 2].reshape(LANES, width)
            vt = cache_ref[i, 1, 2 * p:2 * p + 2].reshape(LANES, width)
            nk = jnp.where(is_new, pltpu.roll(new_ref[p * LANES:(p + 1) * LANES, :], shift, axis=1), 0.0)
            nv = jnp.where(is_new, pltpu.roll(new_ref[ATT_W + p * LANES:ATT_W + (p + 1) * LANES, :], shift, axis=1),
                           0.0)
            for kv, old, new in ((0, kt, nk), (1, vt, nv)):
                rolled = pltpu.roll(old, width - t_new, axis=1)
                last = jnp.where(is_new, new, rolled[:, width - LANES:])
                if width > LANES:
                    last = jnp.concatenate([rolled[:, :width - LANES], last], axis=1)
                cache_out_ref[i, kv, 2 * p:2 * p + 2] = last.reshape(2, HEAD_DIM, width)
            kext = jnp.concatenate([kt, nk], axis=1).astype(BF16)
            vext = jnp.concatenate([vt, nv], axis=1).astype(BF16)
            q = q_ref[i, :, sl]
            q2 = jnp.where(mine, jnp.concatenate([q, q], axis=0), 0.0).astype(BF16)
            s = jnp.where(ok, jnp.dot(q2, kext, preferred_element_type=F32), -jnp.inf)
            m = jnp.max(s, axis=-1, keepdims=True)
            pr = jnp.exp(s - m)
            l = jnp.sum(pr, axis=-1, keepdims=True)
            acc = lax.dot_general(pr.astype(BF16), vext, NT, preferred_element_type=F32) * (1.0 / l)
            lse = m + jnp.log(l)
            low = lane[:tq] < HEAD_DIM
            o_ref[i, :, sl] = jnp.where(low, acc[:tq], acc[tq:])
            lse_ref[i, :, sl] = jnp.where(low, lse[:tq], lse[tq:])


def _attn_sample(cache_t, q_pad, new_t, g, dil, t_new, bb):
    db, _, _, _, width = cache_t.shape
    tq = q_pad.shape[1]
    assert dil & (dil - 1) == 0 and tq & (tq - 1) == 0 and LANES % t_new == 0 and (LANES // t_new) % bb == 0
    per_tile = LANES // t_new
    cblk = pl.BlockSpec((bb, 2, H_G, HEAD_DIM, width), lambda i: (i, 0, 0, 0, 0))
    oblk = pl.BlockSpec((bb, tq, ATT_W), lambda i: (i, 0, 0))
    return pl.pallas_call(
        functools.partial(_attn_sample_kernel, dil=dil, width=width, t_new=t_new, bb=bb),
        grid=(db // bb,),
        in_specs=[cblk, pl.BlockSpec((bb, tq, ATT_W), lambda i: (i, 0, g)),
                  pl.BlockSpec((None, 2 * ATT_W, LANES), lambda i: (g, 0, (i * bb) // per_tile))],
        out_specs=[cblk, oblk, oblk],
        out_shape=[jax.ShapeDtypeStruct(cache_t.shape, F32), jax.ShapeDtypeStruct((db, tq, ATT_W), F32),
                   jax.ShapeDtypeStruct((db, tq, ATT_W), F32)],
        compiler_params=_params(("parallel",)),
        name="attn_sample_d%d" % dil,
    )(cache_t, q_pad, new_t)


CONV_LANES = 256


def _conv_sample_kernel(u_ref, st_ref, cw_ref, conv_ref, st_out_ref, *, db, t_new):
    hist = CONV_WIDTH - 1

    def slab(i):
        return st_ref[i] if i < hist else u_ref[(i - hist) * db:(i - hist + 1) * db, :]

    for t in range(t_new):
        acc = jnp.zeros((db, CONV_LANES), F32)
        for j in range(CONV_WIDTH):
            acc = acc + cw_ref[j:j + 1, :] * slab(t + j)
        conv_ref[t * db:(t + 1) * db, :] = acc
    for i in range(hist):
        st_out_ref[i] = slab(i + t_new)


def _conv_sample(u, state_t, cw, *, db, t_new):
    hist = CONV_WIDTH - 1
    st = pl.BlockSpec((hist, db, CONV_LANES), lambda c: (0, 0, c))
    col = lambda rows: pl.BlockSpec((rows, CONV_LANES), lambda c: (0, c))
    return pl.pallas_call(
        functools.partial(_conv_sample_kernel, db=db, t_new=t_new),
        grid=(C_CONV // CONV_LANES,),
        in_specs=[col(t_new * db), st, col(cw.shape[0])],
        out_specs=[col(t_new * db), st],
        out_shape=[jax.ShapeDtypeStruct(u.shape, F32), jax.ShapeDtypeStruct(state_t.shape, F32)],
        compiler_params=_params(("parallel",)),
        name="conv_sample",
    )(u, state_t, cw)


def kernel(x_prompt, x_sample, cache_kv_w128, cache_kv_w512, cache_kv_w2048, state_conv, ffn1_norm, ffn1_w_gate,
           ffn1_w_up, ffn1_w_down, mix_norm, w_in, gate_bias, conv_w, conv_b, conv_ln_g, conv_ln_b, w_conv_out,
           w_att_out, w_o, ffn2_norm, ffn2_w_gate, ffn2_w_up, ffn2_w_down, final_norm):
    assert ffn1_norm.shape[0] == 1, "single layer"
    b, s, _ = x_prompt.shape
    db, t_new, _ = x_sample.shape
    caches = (cache_kv_w128, cache_kv_w512, cache_kv_w2048)
    bf = lambda w: w[0].astype(BF16)
    ffn1 = (ffn1_norm, bf(ffn1_w_gate), bf(ffn1_w_up), bf(ffn1_w_down))
    ffn2 = (ffn2_norm, bf(ffn2_w_gate), bf(ffn2_w_up), bf(ffn2_w_down))
    w_in_b, wco, wao, wo = bf(w_in), bf(w_conv_out), bf(w_att_out), bf(w_o)
    w_att = w_in[0, :, :N_ATT_COLS].astype(BF16)
    w_cg = w_in[0, :, N_ATT_COLS:].astype(BF16)
    w_kv_t = jnp.stack([w_in[0, :, (3 * g + 1) * ATT_W:(3 * g + 3) * ATT_W].T for g in range(N_GROUPS)]).astype(BF16)
    cw = jnp.pad(conv_w[0], ((0, HALO - CONV_WIDTH), (0, 0)))
    fin = final_norm[None, :]
    tail_w = (conv_b, conv_ln_g, conv_ln_b, wco, wao, wo)

    tm = 512
    y1p = _ffn(x_prompt.reshape(b * s, D_MODEL), *ffn1, fin, final=False, tm=tm)
    y1p3 = y1p.reshape(b, s, D_MODEL)
    cos_p, sin_p = _rope_tables(jnp.arange(s, dtype=jnp.int32))
    q0, q1, q2, kt0, kt1, kt2 = _proj_att(y1p3, mix_norm, w_att, cos_p, sin_p, tm=tm)
    conv_p, gates_p, utail = _proj_conv(y1p3, mix_norm, w_cg, gate_bias, cw, tm=tm)
    att_p = _attn_fused((q0, q1, q2))
    y2p = _tail(y1p, (att_p.reshape(b * s, ATT_W),), conv_p.reshape(b * s, C_CONV),
                gates_p.reshape(b * s, 2 * D_MODEL), *tail_w, tm=tm)
    y_prompt = _ffn(y2p, *ffn2, fin, final=True, tm=tm).reshape(b, s, D_MODEL)
    kv_prompt = [kt.reshape(1, b, kt.shape[1], 2, H_G, HEAD_DIM) for kt in (kt0, kt1, kt2)]
    conv_prompt = utail[None, :, HALO - (CONV_WIDTH - 1):, :]

    n_s = db * t_new
    y1s = _ffn(x_sample.reshape(n_s, D_MODEL), *ffn1, fin, final=False, tm=n_s)
    pos_s = PAST_LEN + jnp.tile(jnp.arange(t_new, dtype=jnp.int32), db)
    cos_s, sin_s = _rope_tables(pos_s)
    q_s, u_s, gates_s = _proj_sample(y1s, mix_norm, w_in_b, gate_bias, cos_s, sin_s, tm=n_s // 2)
    new_t = _proj_kvt(y1s, mix_norm, w_kv_t, cos_s.T, sin_s.T)
    q_pad = jnp.pad(q_s.reshape(db, t_new, N_GROUPS * ATT_W), ((0, 0), (0, SAMPLE_Q_ROWS - t_new), (0, 0)))
    att_args, lses, kv_sample = [], [], []
    for g, (win, dil) in enumerate(GROUPS):
        cache_t = jnp.transpose(caches[g][0], (0, 2, 3, 4, 1))
        bb = max(1, min(8, 1024 // cache_t.shape[-1]))
        shifted, o, l = _attn_sample(cache_t, q_pad, new_t, g, dil, t_new, bb)
        kv_sample.append(jnp.transpose(shifted, (0, 4, 1, 2, 3))[None])
        att_args.append(o[:, :t_new].reshape(n_s, ATT_W))
        lses.append(l[:, :t_new].reshape(n_s, ATT_W))
    to_token_major = lambda a: jnp.swapaxes(a.reshape(db, t_new, -1), 0, 1).reshape(n_s, -1)
    state_t = jnp.swapaxes(state_conv[0], 0, 1)
    conv_tm, state_new = _conv_sample(to_token_major(u_s), state_t, cw, db=db, t_new=t_new)
    conv_s = jnp.swapaxes(conv_tm.reshape(t_new, db, C_CONV), 0, 1).reshape(n_s, C_CONV)
    y2s = _tail(y1s, (*att_args, *lses), conv_s, gates_s, *tail_w, tm=n_s // 2)
    y_sample = _ffn(y2s, *ffn2, fin, final=True, tm=n_s).reshape(db, t_new, D_MODEL)
    conv_sample = jnp.swapaxes(state_new, 0, 1)[None]

    return (y_prompt, y_sample, kv_prompt[0], kv_prompt[1], kv_prompt[2], conv_prompt,
            kv_sample[0], kv_sample[1], kv_sample[2], conv_sample)
```
